```python
import functools
import jax, jax.numpy as jnp
from jax import lax
import numpy as np

D_MODEL = 2048
BATCH = 4
SEQ = 2048
DEPTH = 2
DEC_BATCH = 128
DEC_SEQ = 1
PAST_LEN = 8192
PAGE_SIZE = 128

PLE_DIM = 256
GLA_HEADS = 4
GLA_DK = 128
GLA_DV = 256
GLA_GATE_RANK = 16
GLA_GATE_NORM = 16.0
GLA_CHUNK = 64
MLA_HEADS = 8
MLA_Q_RANK = 512
MLA_KV_RANK = 256
MLA_NOPE = 128
MLA_ROPE = 64
MLA_DV = 128
ROPE_THETA = 10000.0
Q_BLOCK = 128
D_FF = -(-8 * D_MODEL // (3 * 256)) * 256
MIX_WIDTH = GLA_HEADS * GLA_DV + MLA_HEADS * MLA_DV
EPS = 1e-6
IN_SIZES = (GLA_HEADS * GLA_DK, GLA_HEADS * GLA_DK, GLA_HEADS * GLA_DV, GLA_GATE_RANK,
            GLA_HEADS * GLA_DV, MLA_Q_RANK, MLA_KV_RANK, MLA_ROPE)
IN_WIDTH = sum(IN_SIZES)

kernel_name = "hybrid_gla_mla_decoder_step"


def rmsnorm(x, g):
    xf = x.astype(jnp.float32)
    xf = xf * lax.rsqrt(jnp.mean(xf * xf, axis=-1, keepdims=True) + EPS)
    return (xf * g.astype(jnp.float32)).astype(x.dtype)


def rope(x, pos):
    e = x.shape[-1]
    freqs = ROPE_THETA ** (-jnp.arange(0, e, 2, dtype=jnp.float32) / e)
    ang = pos[:, None] * freqs[None, :]
    shp = (pos.shape[0],) + (1,) * (x.ndim - 3) + (e // 2,)
    cos = jnp.cos(ang).reshape(shp)
    sin = jnp.sin(ang).reshape(shp)
    xf = x.astype(jnp.float32)
    x1, x2 = xf[..., : e // 2], xf[..., e // 2:]
    return jnp.concatenate([x1 * cos - x2 * sin, x1 * sin + x2 * cos], axis=-1).astype(x.dtype)


def gla_chunked(q, k, v, gk, s0, chunk):
    B, L, H, DK = q.shape
    DV = v.shape[-1]
    n = L // chunk

    def to_chunks(a):
        return a.reshape(B, n, chunk, H, a.shape[-1]).transpose(1, 0, 3, 2, 4)

    mask = jnp.tril(jnp.ones((chunk, chunk), dtype=bool))[:, :, None]

    def step(S, inp):
        qb, kb, vb, gb = inp
        b = jnp.cumsum(gb, axis=2)
        diff = b[:, :, :, None, :] - b[:, :, None, :, :]
        decay = jnp.exp(jnp.where(mask, diff, -jnp.inf))
        attn = jnp.einsum('bhtd,bhsd,bhtsd->bhts', qb, kb, decay)
        o = (jnp.einsum('bhts,bhsv->bhtv', attn, vb)
             + jnp.einsum('bhtd,bhdv->bhtv', qb * jnp.exp(b), S))
        b_last = b[:, :, -1:, :]
        S_new = (jnp.exp(b_last[:, :, 0, :])[..., None] * S
                 + jnp.einsum('bhsd,bhsv->bhdv', kb * jnp.exp(b_last - b), vb))
        return S_new, o

    S_fin, o = lax.scan(step, s0, (to_chunks(q), to_chunks(k), to_chunks(v), to_chunks(gk)))
    o = o.transpose(1, 0, 3, 2, 4).reshape(B, L, H, DV)
    return o, S_fin


def gla_mixer(q, k, v, glr, g, s0, chunk, w_gk2, b_gk, gla_norm):
    B, L, _ = q.shape
    dt = q.dtype
    q = q.reshape(B, L, GLA_HEADS, GLA_DK) * (GLA_DK ** -0.5)
    k = k.reshape(B, L, GLA_HEADS, GLA_DK)
    v = v.reshape(B, L, GLA_HEADS, GLA_DV)
    gk = jax.nn.log_sigmoid((glr @ w_gk2 + b_gk).astype(jnp.float32)) / GLA_GATE_NORM
    gk = gk.reshape(B, L, GLA_HEADS, GLA_DK)
    f32 = jnp.float32
    o, S = gla_chunked(q.astype(f32), k.astype(f32), v.astype(f32), gk, s0.astype(f32), chunk)
    o = rmsnorm(o, gla_norm).astype(dt) * jax.nn.silu(g.reshape(B, L, GLA_HEADS, GLA_DV))
    return o.reshape(B, L, GLA_HEADS * GLA_DV), S.astype(s0.dtype)


def mla_prompt_attend(q_lat, q_pe, ckv, kpe):
    B, L, H, R = q_lat.shape
    nb = L // Q_BLOCK
    scale = (MLA_NOPE + MLA_ROPE) ** -0.5
    ql = q_lat.reshape(B, nb, Q_BLOCK, H, R).transpose(1, 0, 2, 3, 4)
    qp = q_pe.reshape(B, nb, Q_BLOCK, H, MLA_ROPE).transpose(1, 0, 2, 3, 4)
    kpos = jnp.arange(L)

    def block(args):
        i, qlb, qpb = args
        s = (jnp.einsum('bqhr,bkr->bhqk', qlb, ckv)
             + jnp.einsum('bqhe,bke->bhqk', qpb, kpe)).astype(jnp.float32) * scale
        qpos = i * Q_BLOCK + jnp.arange(Q_BLOCK)
        s = jnp.where(kpos[None, :] <= qpos[:, None], s, -jnp.inf)
        p = jax.nn.softmax(s, axis=-1).astype(ckv.dtype)
        return jnp.einsum('bhqk,bkr->bqhr', p, ckv)

    o = lax.map(block, (jnp.arange(nb), ql, qp))
    return o.transpose(1, 0, 2, 3, 4).reshape(B, L, H, R)


def mla_sample_attend(q_lat, q_pe, ckv, kpe, ckv_past, kpe_past):
    Lq = q_lat.shape[1]
    P = ckv_past.shape[1]
    scale = (MLA_NOPE + MLA_ROPE) ** -0.5
    s_past = (jnp.einsum('bqhr,bkr->bhqk', q_lat, ckv_past)
              + jnp.einsum('bqhe,bke->bhqk', q_pe, kpe_past)).astype(jnp.float32)
    s_new = (jnp.einsum('bqhr,bkr->bhqk', q_lat, ckv)
             + jnp.einsum('bqhe,bke->bhqk', q_pe, kpe)).astype(jnp.float32)
    causal = jnp.tril(jnp.ones((Lq, Lq), dtype=bool))[None, None]
    s_new = jnp.where(causal, s_new, -jnp.inf)
    p = jax.nn.softmax(jnp.concatenate([s_past, s_new], axis=-1) * scale, axis=-1).astype(ckv.dtype)
    return (jnp.einsum('bhqk,bkr->bqhr', p[..., :P], ckv_past)
            + jnp.einsum('bhqk,bkr->bqhr', p[..., P:], ckv))


def trunk_layer(x, p_i, pos, gla_chunk, s0, attend,
                norm_mix, w_in, w_gk2, b_gk, gla_norm, q_a_norm, kv_a_norm, w_qb, w_kb, w_vb, w_o,
                norm_ffn, w_ffn_in, w_ffn_out, norm_ple, w_ple_gate, w_ple_proj):
    B, L, _ = x.shape
    h = rmsnorm(x, norm_mix)
    z = h @ w_in
    split_idx = tuple(int(c) for c in np.cumsum(IN_SIZES)[:-1])
    q_a, k_a, v_a, glr, g_a, c_q, c_kv, k_pe = jnp.split(z, split_idx, axis=-1)
    o_a, s_new = gla_mixer(q_a, k_a, v_a, glr, g_a, s0, gla_chunk, w_gk2, b_gk, gla_norm)
    qb = (rmsnorm(c_q, q_a_norm) @ w_qb).reshape(B, L, MLA_HEADS, MLA_NOPE + MLA_ROPE)
    q_nope, q_pe = qb[..., :MLA_NOPE], rope(qb[..., MLA_NOPE:], pos)
    ckv = rmsnorm(c_kv, kv_a_norm)
    kpe = rope(k_pe, pos)
    q_lat = jnp.einsum('blhn,rhn->blhr', q_nope, w_kb)
    o_lat = attend(q_lat, q_pe, ckv, kpe)
    o_b = jnp.einsum('blhr,rhv->blhv', o_lat, w_vb).reshape(B, L, MLA_HEADS * MLA_DV)
    x = x + jnp.concatenate([o_a, o_b], axis=-1) @ w_o
    gu = rmsnorm(x, norm_ffn) @ w_ffn_in
    x = x + (jax.nn.silu(gu[..., :D_FF]) * gu[..., D_FF:]) @ w_ffn_out
    gate = jax.nn.sigmoid(rmsnorm(x, norm_ple) @ w_ple_gate)
    x = x + gate * (p_i @ w_ple_proj)
    return x, s_new, ckv, kpe


def setup_inputs(seed: int = 0) -> dict:
    key = jax.random.key(seed)
    ks = jax.random.split(key, 32)
    f32 = jnp.float32
    n_pages = PAST_LEN // PAGE_SIZE
    n_pool = (DEC_BATCH * n_pages * 5) // 4

    def nrm(k, shape, scale):
        return jax.random.normal(k, shape, f32) * scale

    def gain(k, shape):
        return 1.0 + 0.01 * jax.random.normal(k, shape, f32)

    page_table = jax.random.permutation(ks[0], n_pool)[: DEC_BATCH * n_pages]
    page_table = page_table.reshape(DEC_BATCH, n_pages).astype(jnp.int32)
    return {
        "x_prompt": nrm(ks[1], (BATCH, SEQ, D_MODEL), 1.0),
        "x_sample": nrm(ks[2], (DEC_BATCH, DEC_SEQ, D_MODEL), 1.0),
        "state_gla": nrm(ks[3], (DEPTH, DEC_BATCH, GLA_HEADS, GLA_DK, GLA_DV), 0.5),
        "cache_ckv": nrm(ks[4], (DEPTH, n_pool, PAGE_SIZE, MLA_KV_RANK), 1.0),
        "cache_kpe": nrm(ks[5], (DEPTH, n_pool, PAGE_SIZE, MLA_ROPE), 1.0),
        "page_table": page_table,
        "p_prompt": nrm(ks[6], (DEPTH, BATCH, SEQ, PLE_DIM), 1.0),
        "p_sample": nrm(ks[7], (DEPTH, DEC_BATCH, DEC_SEQ, PLE_DIM), 1.0),
        "norm_mix": gain(ks[8], (DEPTH, D_MODEL)),
        "w_in": nrm(ks[9], (DEPTH, D_MODEL, IN_WIDTH), D_MODEL ** -0.5),
        "w_gk2": nrm(ks[10], (DEPTH, GLA_GATE_RANK, GLA_HEADS * GLA_DK), GLA_GATE_RANK ** -0.5),
        "b_gk": nrm(ks[11], (DEPTH, GLA_HEADS * GLA_DK), 0.01),
        "gla_norm": gain(ks[12], (DEPTH, GLA_DV)),
        "q_a_norm": gain(ks[13], (DEPTH, MLA_Q_RANK)),
        "kv_a_norm": gain(ks[14], (DEPTH, MLA_KV_RANK)),
        "w_qb": nrm(ks[15], (DEPTH, MLA_Q_RANK, MLA_HEADS * (MLA_NOPE + MLA_ROPE)), MLA_Q_RANK ** -0.5),
        "w_kb": nrm(ks[16], (DEPTH, MLA_KV_RANK, MLA_HEADS, MLA_NOPE), MLA_KV_RANK ** -0.5),
        "w_vb": nrm(ks[17], (DEPTH, MLA_KV_RANK, MLA_HEADS, MLA_DV), MLA_KV_RANK ** -0.5),
        "w_o": nrm(ks[18], (DEPTH, MIX_WIDTH, D_MODEL), MIX_WIDTH ** -0.5),
        "norm_ffn": gain(ks[19], (DEPTH, D_MODEL)),
        "w_ffn_in": nrm(ks[20], (DEPTH, D_MODEL, 2 * D_FF), D_MODEL ** -0.5),
        "w_ffn_out": nrm(ks[21], (DEPTH, D_FF, D_MODEL), D_FF ** -0.5),
        "norm_ple": gain(ks[22], (DEPTH, D_MODEL)),
        "w_ple_gate": nrm(ks[23], (DEPTH, D_MODEL, D_MODEL), D_MODEL ** -0.5),
        "w_ple_proj": nrm(ks[24], (DEPTH, PLE_DIM, D_MODEL), PLE_DIM ** -0.5),
        "norm_final": gain(ks[25], (D_MODEL,)),
    }


def reference(x_prompt, x_sample, state_gla, cache_ckv, cache_kpe, page_table, p_prompt, p_sample,
              norm_mix, w_in, w_gk2, b_gk, gla_norm, q_a_norm, kv_a_norm, w_qb, w_kb, w_vb, w_o,
              norm_ffn, w_ffn_in, w_ffn_out, norm_ple, w_ple_gate, w_ple_proj, norm_final):
    B, L, _ = x_prompt.shape
    DB, LS, _ = x_sample.shape
    past = page_table.shape[1] * cache_ckv.shape[2]
    pos_prompt = jnp.arange(L, dtype=jnp.float32)
    pos_sample = past + jnp.arange(LS, dtype=jnp.float32)
    s0_prompt = jnp.zeros((B, GLA_HEADS, GLA_DK, GLA_DV), state_gla.dtype)

    xp, xs = x_prompt, x_sample
    gla_p, gla_s, ckv_p, kpe_p, ckv_s, kpe_s = [], [], [], [], [], []
    for i in range(DEPTH):
        w_i = (norm_mix[i], w_in[i], w_gk2[i], b_gk[i], gla_norm[i], q_a_norm[i], kv_a_norm[i],
               w_qb[i], w_kb[i], w_vb[i], w_o[i], norm_ffn[i], w_ffn_in[i], w_ffn_out[i],
               norm_ple[i], w_ple_gate[i], w_ple_proj[i])
        xp, sp, cp, kp = trunk_layer(xp, p_prompt[i], pos_prompt, GLA_CHUNK, s0_prompt,
                                     mla_prompt_attend, *w_i)
        ckv_past = cache_ckv[i][page_table].reshape(DB, past, MLA_KV_RANK)
        kpe_past = cache_kpe[i][page_table].reshape(DB, past, MLA_ROPE)
        attend_s = functools.partial(mla_sample_attend, ckv_past=ckv_past, kpe_past=kpe_past)
        xs, ss, cs, ks_ = trunk_layer(xs, p_sample[i], pos_sample, LS, state_gla[i],
                                      attend_s, *w_i)
        gla_p.append(sp); gla_s.append(ss)
        ckv_p.append(cp); kpe_p.append(kp)
        ckv_s.append(cs); kpe_s.append(ks_)

    y_prompt = rmsnorm(xp, norm_final)
    y_sample = rmsnorm(xs, norm_final)
    return (y_prompt, y_sample, jnp.stack(gla_p), jnp.stack(gla_s), jnp.stack(ckv_p),
            jnp.stack(kpe_p), jnp.stack(ckv_s), jnp.stack(kpe_s))
```

```python
import functools

import jax
import jax.numpy as jnp
import numpy as np
from jax import lax
from jax.experimental import pallas as pl
from jax.experimental.pallas import tpu as pltpu

F32 = jnp.float32
BF16 = jnp.bfloat16

GLA_HEADS = 4
GLA_DK = 128
GLA_DV = 256
GLA_GATE_RANK = 16
GLA_GATE_NORM = 16.0
MLA_HEADS = 8
MLA_Q_RANK = 512
MLA_KV_RANK = 256
MLA_NOPE = 128
MLA_ROPE = 64
MLA_DV = 128
ROPE_THETA = 10000.0
EPS = 1e-6

QK_WIDTH = MLA_KV_RANK + MLA_ROPE
Q_HEAD_COLS = MLA_NOPE + 2 * MLA_ROPE

Z_Q = 0
Z_K = Z_Q + GLA_HEADS * GLA_DK
Z_V = Z_K + GLA_HEADS * GLA_DK
Z_G = Z_V + GLA_HEADS * GLA_DV
Z_CQ = Z_G + GLA_HEADS * GLA_DV
Z_CKV = Z_CQ + MLA_Q_RANK
Z_TAIL = Z_CKV + MLA_KV_RANK
TAIL_W = 256
TAIL_KPE = 0
TAIL_KPE_ROT = MLA_ROPE
TAIL_GLR = 2 * MLA_ROPE
Z_WIDTH = Z_TAIL + TAIL_W

VMEM_LIMIT_BYTES = 56 * 1024 * 1024

GLA_CHUNK = 128
GLA_SUB = 16
PAGES_PER_CHUNK = 16


def _cparams(*sem):
    return pltpu.CompilerParams(dimension_semantics=sem, vmem_limit_bytes=VMEM_LIMIT_BYTES)


def _rmsnorm(x, g):
    ms = jnp.mean(x * x, axis=-1, keepdims=True)
    return x * lax.rsqrt(ms + EPS) * g


def _sigmoid(x):
    return 1.0 / (1.0 + jnp.exp(-x))


def _log_sigmoid(x):
    return jnp.minimum(x, 0.0) - jnp.log1p(jnp.exp(-jnp.abs(x)))


def _dot(a, b):
    return jnp.dot(a, b, preferred_element_type=F32)


def _dot_nt(a, b):
    return lax.dot_general(a, b, (((1,), (1,)), ((), ())), preferred_element_type=F32)


def _dot_tn(a, b):
    return lax.dot_general(a, b, (((0,), (0,)), ((), ())), preferred_element_type=F32)


def _norm_mm_kernel(x_ref, g_ref, w_ref, o_ref, h_ref):
    @pl.when(pl.program_id(1) == 0)
    def _():
        h_ref[...] = _rmsnorm(x_ref[...], g_ref[...]).astype(BF16)

    o_ref[...] = _dot(h_ref[...], w_ref[...])


def norm_matmul(x, g, w, tm, tn):
    m, k = x.shape
    n = w.shape[1]
    return pl.pallas_call(
        _norm_mm_kernel,
        grid=(m // tm, n // tn),
        in_specs=[
            pl.BlockSpec((tm, k), lambda i, j: (i, 0)),
            pl.BlockSpec((1, k), lambda i, j: (0, 0)),
            pl.BlockSpec((k, tn), lambda i, j: (0, j)),
        ],
        out_specs=pl.BlockSpec((tm, tn), lambda i, j: (i, j)),
        out_shape=jax.ShapeDtypeStruct((m, n), F32),
        scratch_shapes=[pltpu.VMEM((tm, k), BF16)],
        compiler_params=_cparams("parallel", "arbitrary"),
        name="in_proj",
    )(x, g.reshape(1, k), w)


def _ffn_in_kernel(x_ref, g_ref, wg_ref, wu_ref, o_ref, h_ref):
    @pl.when(pl.program_id(1) == 0)
    def _():
        h_ref[...] = _rmsnorm(x_ref[...], g_ref[...]).astype(BF16)

    h = h_ref[...]
    a = _dot(h, wg_ref[...])
    u = _dot(h, wu_ref[...])
    o_ref[...] = (a * _sigmoid(a) * u).astype(BF16)


def ffn_in(x, g, w, tm, tn):
    m, k = x.shape
    d_ff = w.shape[1] // 2
    nj = d_ff // tn
    return pl.pallas_call(
        _ffn_in_kernel,
        grid=(m // tm, nj),
        in_specs=[
            pl.BlockSpec((tm, k), lambda i, j: (i, 0)),
            pl.BlockSpec((1, k), lambda i, j: (0, 0)),
            pl.BlockSpec((k, tn), lambda i, j: (0, j)),
            pl.BlockSpec((k, tn), lambda i, j: (0, j + nj)),
        ],
        out_specs=pl.BlockSpec((tm, tn), lambda i, j: (i, j)),
        out_shape=jax.ShapeDtypeStruct((m, d_ff), BF16),
        scratch_shapes=[pltpu.VMEM((tm, k), BF16)],
        compiler_params=_cparams("parallel", "arbitrary"),
        name="ffn_in",
    )(x, g.reshape(1, k), w, w)


def _mm_res_kernel(*refs):
    res_ref, o_ref = refs[0], refs[-1]
    acc = res_ref[...]
    for a_ref, w_ref in zip(refs[1:-1:2], refs[2:-1:2]):
        acc = acc + _dot(a_ref[...], w_ref[...])
    o_ref[...] = acc


def matmul_residual(res, pairs, tm, tn, name):
    m, n = res.shape
    in_specs = [pl.BlockSpec((tm, tn), lambda i, j: (i, j))]
    args = [res]
    for a, w in pairs:
        k = a.shape[1]
        in_specs.append(pl.BlockSpec((tm, k), lambda i, j: (i, 0)))
        in_specs.append(pl.BlockSpec((k, tn), lambda i, j: (0, j)))
        args += [a, w]
    return pl.pallas_call(
        _mm_res_kernel,
        grid=(m // tm, n // tn),
        in_specs=in_specs,
        out_specs=pl.BlockSpec((tm, tn), lambda i, j: (i, j)),
        out_shape=jax.ShapeDtypeStruct((m, n), F32),
        compiler_params=_cparams("parallel", "arbitrary"),
        name=name,
    )(*args)


def _ple_kernel(x_ref, g_ref, wg_ref, p_ref, wp_ref, xres_ref, o_ref, h_ref):
    @pl.when(pl.program_id(1) == 0)
    def _():
        h_ref[...] = _rmsnorm(x_ref[...], g_ref[...]).astype(BF16)

    gate = _sigmoid(_dot(h_ref[...], wg_ref[...]))
    emb = _dot(p_ref[...].astype(BF16), wp_ref[...])
    o_ref[...] = xres_ref[...] + gate * emb


def ple(x, g, wg, p, wp, tm, tn):
    m, k = x.shape
    n = wg.shape[1]
    kp = p.shape[1]
    return pl.pallas_call(
        _ple_kernel,
        grid=(m // tm, n // tn),
        in_specs=[
            pl.BlockSpec((tm, k), lambda i, j: (i, 0)),
            pl.BlockSpec((1, k), lambda i, j: (0, 0)),
            pl.BlockSpec((k, tn), lambda i, j: (0, j)),
            pl.BlockSpec((tm, kp), lambda i, j: (i, 0)),
            pl.BlockSpec((kp, tn), lambda i, j: (0, j)),
            pl.BlockSpec((tm, tn), lambda i, j: (i, j)),
        ],
        out_specs=pl.BlockSpec((tm, tn), lambda i, j: (i, j)),
        out_shape=jax.ShapeDtypeStruct((m, n), F32),
        scratch_shapes=[pltpu.VMEM((tm, k), BF16)],
        compiler_params=_cparams("parallel", "arbitrary"),
        name="ple",
    )(x, g.reshape(1, k), wg, p, wp, x)


def _final_norm_kernel(x_ref, g_ref, o_ref):
    o_ref[...] = _rmsnorm(x_ref[...], g_ref[...])


def final_norm(x, g, tm):
    m, k = x.shape
    return pl.pallas_call(
        _final_norm_kernel,
        grid=(m // tm,),
        in_specs=[pl.BlockSpec((tm, k), lambda i: (i, 0)), pl.BlockSpec((1, k), lambda i: (0, 0))],
        out_specs=pl.BlockSpec((tm, k), lambda i: (i, 0)),
        out_shape=jax.ShapeDtypeStruct((m, k), F32),
        compiler_params=_cparams("parallel"),
        name="final_norm",
    )(x, g.reshape(1, k))


def _mla_prep_kernel(cq_ref, ckv_ref, tail_ref, cos_ref, sin_ref, qn_ref, kvn_ref, wqb_ref, wkb_ref,
                     q_ref, ckv_out_ref, kpe_out_ref, kvcat_ref):
    scale = (MLA_NOPE + MLA_ROPE) ** -0.5
    cos = cos_ref[...]
    sin = sin_ref[...]
    cqn = _rmsnorm(cq_ref[...], qn_ref[...]).astype(BF16)
    qb = _dot(cqn, wqb_ref[...])
    for h in range(MLA_HEADS):
        base = h * Q_HEAD_COLS
        nope = qb[:, base:base + MLA_NOPE]
        pe = qb[:, base + MLA_NOPE:base + MLA_NOPE + MLA_ROPE]
        pe_rot = qb[:, base + MLA_NOPE + MLA_ROPE:base + Q_HEAD_COLS]
        q_lat = _dot(nope.astype(BF16), wkb_ref[h])
        q_pe = pe * cos + pe_rot * sin
        q_ref[h, :, 0:MLA_KV_RANK] = (q_lat * scale).astype(BF16)
        q_ref[h, :, MLA_KV_RANK:QK_WIDTH] = (q_pe * scale).astype(BF16)
    ckv = _rmsnorm(ckv_ref[...], kvn_ref[...])
    tail = tail_ref[...]
    kpe = (tail[:, TAIL_KPE:TAIL_KPE + MLA_ROPE] * cos
           + tail[:, TAIL_KPE_ROT:TAIL_KPE_ROT + MLA_ROPE] * sin)
    ckv_out_ref[...] = ckv
    kpe_out_ref[...] = kpe
    kvcat_ref[:, 0:MLA_KV_RANK] = ckv.astype(BF16)
    kvcat_ref[:, MLA_KV_RANK:QK_WIDTH] = kpe.astype(BF16)


def mla_prep(z, cos, sin, q_a_norm, kv_a_norm, wqb, wkb_t, tm):
    m = z.shape[0]
    n_pos_blocks = cos.shape[0] // tm
    return pl.pallas_call(
        _mla_prep_kernel,
        grid=(m // tm,),
        in_specs=[
            pl.BlockSpec((tm, MLA_Q_RANK), lambda i: (i, Z_CQ // MLA_Q_RANK)),
            pl.BlockSpec((tm, MLA_KV_RANK), lambda i: (i, Z_CKV // MLA_KV_RANK)),
            pl.BlockSpec((tm, TAIL_W), lambda i: (i, Z_TAIL // TAIL_W)),
            pl.BlockSpec((tm, MLA_ROPE), lambda i: (i % n_pos_blocks, 0)),
            pl.BlockSpec((tm, MLA_ROPE), lambda i: (i % n_pos_blocks, 0)),
            pl.BlockSpec((1, MLA_Q_RANK), lambda i: (0, 0)),
            pl.BlockSpec((1, MLA_KV_RANK), lambda i: (0, 0)),
            pl.BlockSpec(wqb.shape, lambda i: (0, 0)),
            pl.BlockSpec(wkb_t.shape, lambda i: (0, 0, 0)),
        ],
        out_specs=[
            pl.BlockSpec((MLA_HEADS, tm, QK_WIDTH), lambda i: (0, i, 0)),
            pl.BlockSpec((tm, MLA_KV_RANK), lambda i: (i, 0)),
            pl.BlockSpec((tm, MLA_ROPE), lambda i: (i, 0)),
            pl.BlockSpec((tm, QK_WIDTH), lambda i: (i, 0)),
        ],
        out_shape=[
            jax.ShapeDtypeStruct((MLA_HEADS, m, QK_WIDTH), BF16),
            jax.ShapeDtypeStruct((m, MLA_KV_RANK), F32),
            jax.ShapeDtypeStruct((m, MLA_ROPE), F32),
            jax.ShapeDtypeStruct((m, QK_WIDTH), BF16),
        ],
        compiler_params=_cparams("parallel"),
        name="mla_prep",
    )(z, z, z, cos, sin, q_a_norm.reshape(1, -1), kv_a_norm.reshape(1, -1), wqb, wkb_t)


def _mla_prompt_kernel(q_ref, kv_ref, wvb_ref, o_ref, m_ref, l_ref, acc_ref, *, tq):
    qi = pl.program_id(1)
    rows = MLA_HEADS * tq
    q = q_ref[...].reshape(rows, QK_WIDTH)
    m_ref[...] = jnp.full((rows, 1), -jnp.inf, F32)
    l_ref[...] = jnp.zeros((rows, 1), F32)
    acc_ref[...] = jnp.zeros((rows, MLA_KV_RANK), F32)

    def block(kb, masked):
        kv = kv_ref[pl.ds(pl.multiple_of(kb * tq, tq), tq), :]
        s = _dot_nt(q, kv)
        if masked:
            t = lax.broadcasted_iota(jnp.int32, (rows, tq), 0) & (tq - 1)
            c = lax.broadcasted_iota(jnp.int32, (rows, tq), 1)
            s = jnp.where(c <= t, s, -jnp.inf)
        m_prev = m_ref[...]
        m_new = jnp.maximum(m_prev, jnp.max(s, axis=-1, keepdims=True))
        alpha = jnp.exp(m_prev - m_new)
        p = jnp.exp(s - m_new)
        l_ref[...] = alpha * l_ref[...] + jnp.sum(p, axis=-1, keepdims=True)
        acc_ref[...] = alpha * acc_ref[...] + _dot(p.astype(BF16), kv[:, 0:MLA_KV_RANK])
        m_ref[...] = m_new

    def body(kb, carry):
        block(kb, False)
        return carry

    lax.fori_loop(0, qi, body, 0)
    block(qi, True)

    o_lat = acc_ref[...] / l_ref[...]
    for h in range(MLA_HEADS):
        o_h = o_lat[h * tq:(h + 1) * tq].astype(BF16)
        o_ref[:, h * MLA_DV:(h + 1) * MLA_DV] = _dot(o_h, wvb_ref[h]).astype(BF16)


def mla_prompt_attention(q, kvcat, wvb, batch, seq, tq):
    nq = seq // tq
    rows = MLA_HEADS * tq
    return pl.pallas_call(
        functools.partial(_mla_prompt_kernel, tq=tq),
        grid=(batch, nq),
        in_specs=[
            pl.BlockSpec((MLA_HEADS, tq, QK_WIDTH), lambda b, i: (0, b * nq + i, 0)),
            pl.BlockSpec((seq, QK_WIDTH), lambda b, i: (b, 0)),
            pl.BlockSpec(wvb.shape, lambda b, i: (0, 0, 0)),
        ],
        out_specs=pl.BlockSpec((tq, MLA_HEADS * MLA_DV), lambda b, i: (b * nq + i, 0)),
        out_shape=jax.ShapeDtypeStruct((batch * seq, MLA_HEADS * MLA_DV), BF16),
        scratch_shapes=[
            pltpu.VMEM((rows, 1), F32),
            pltpu.VMEM((rows, 1), F32),
            pltpu.VMEM((rows, MLA_KV_RANK), F32),
        ],
        compiler_params=_cparams("parallel", "arbitrary"),
        name="mla_prompt_attn",
    )(q, kvcat, wvb)


def _mla_decode_kernel(pt_ref, q_ref, kvn_ref, wvb_ref, ckv_hbm, kpe_hbm, o_ref,
                       ckv_buf, kpe_buf, sem, olat_ref, *, layer, page, n_chunks):
    b = pl.program_id(0)
    nb = pl.num_programs(0)

    def copies(bb, c, slot):
        out = []
        for j in range(PAGES_PER_CHUNK):
            pg = pt_ref[bb, c * PAGES_PER_CHUNK + j]
            rows = pl.ds(j * page, page)
            out.append(pltpu.make_async_copy(ckv_hbm.at[layer, pg], ckv_buf.at[slot, rows], sem.at[slot]))
            out.append(pltpu.make_async_copy(kpe_hbm.at[layer, pg], kpe_buf.at[slot, rows], sem.at[slot]))
        return out

    @pl.when(b == 0)
    def _():
        for cp in copies(0, 0, 0):
            cp.start()

    q = q_ref[0].astype(F32)
    q_lat = q[:, 0:MLA_KV_RANK]
    q_pe = q[:, MLA_KV_RANK:QK_WIDTH]
    kvn = kvn_ref[0].astype(F32)
    m = jnp.sum(q * kvn, axis=-1, keepdims=True)
    l = jnp.ones_like(m)
    acc = jnp.broadcast_to(kvn[:, 0:MLA_KV_RANK], (MLA_HEADS, MLA_KV_RANK))

    for c in range(n_chunks):
        slot = c % 2
        if c + 1 < n_chunks:
            for cp in copies(b, c + 1, 1 - slot):
                cp.start()
        else:
            @pl.when(b + 1 < nb)
            def _():
                for cp in copies(b + 1, 0, 1 - slot):
                    cp.start()
        for cp in copies(b, c, slot):
            cp.wait()
        ckv = ckv_buf[slot]
        kpe = kpe_buf[slot]
        s = _dot_nt(q_lat, ckv) + _dot_nt(q_pe, kpe)
        m_new = jnp.maximum(m, jnp.max(s, axis=-1, keepdims=True))
        alpha = jnp.exp(m - m_new)
        p = jnp.exp(s - m_new)
        l = alpha * l + jnp.sum(p, axis=-1, keepdims=True)
        acc = alpha * acc + _dot(p, ckv)
        m = m_new

    o_lat = acc / l
    for h in range(MLA_HEADS):
        olat_ref[h, pl.ds(b, 1), :] = o_lat[h:h + 1]

    @pl.when(b == nb - 1)
    def _():
        for h in range(MLA_HEADS):
            o_ref[:, h * MLA_DV:(h + 1) * MLA_DV] = _dot(olat_ref[h].astype(BF16), wvb_ref[h]).astype(BF16)


def mla_decode_attention(page_table, q, kvcat_new, wvb, cache_ckv, cache_kpe, layer):
    nb, n_pages = page_table.shape
    page = cache_ckv.shape[2]
    assert n_pages % (2 * PAGES_PER_CHUNK) == 0
    n_chunks = n_pages // PAGES_PER_CHUNK
    keys = PAGES_PER_CHUNK * page
    grid_spec = pltpu.PrefetchScalarGridSpec(
        num_scalar_prefetch=1,
        grid=(nb,),
        in_specs=[
            pl.BlockSpec((1, MLA_HEADS, QK_WIDTH), lambda b, pt: (b, 0, 0)),
            pl.BlockSpec((1, 1, QK_WIDTH), lambda b, pt: (b, 0, 0)),
            pl.BlockSpec(wvb.shape, lambda b, pt: (0, 0, 0)),
            pl.BlockSpec(memory_space=pl.ANY),
            pl.BlockSpec(memory_space=pl.ANY),
        ],
        out_specs=pl.BlockSpec((nb, MLA_HEADS * MLA_DV), lambda b, pt: (0, 0)),
        scratch_shapes=[
            pltpu.VMEM((2, keys, MLA_KV_RANK), F32),
            pltpu.VMEM((2, keys, MLA_ROPE), F32),
            pltpu.SemaphoreType.DMA((2,)),
            pltpu.VMEM((MLA_HEADS, nb, MLA_KV_RANK), F32),
        ],
    )
    return pl.pallas_call(
        functools.partial(_mla_decode_kernel, layer=layer, page=page, n_chunks=n_chunks),
        grid_spec=grid_spec,
        out_shape=jax.ShapeDtypeStruct((nb, MLA_HEADS * MLA_DV), BF16),
        compiler_params=_cparams("arbitrary"),
        name="mla_decode_attn",
    )(page_table, q, kvcat_new.reshape(nb, 1, QK_WIDTH), wvb, cache_ckv, cache_kpe)


def _gla_prompt_kernel(q_ref, k_ref, v_ref, g_ref, tail_ref, wgk_ref, bgk_ref, gn_ref,
                       o_ref, sfin_ref, st_ref, *, ct):
    ci = pl.program_id(2)

    @pl.when(ci == 0)
    def _():
        st_ref[...] = jnp.zeros_like(st_ref)

    pre = _dot(tail_ref[...].astype(BF16), wgk_ref[...]) + bgk_ref[...]
    gk_all = _log_sigmoid(pre) * (1.0 / GLA_GATE_NORM)
    row = lax.broadcasted_iota(jnp.int32, (GLA_CHUNK, GLA_CHUNK), 0)
    col = lax.broadcasted_iota(jnp.int32, (GLA_CHUNK, GLA_CHUNK), 1)
    causal = col <= row
    tri = causal.astype(F32)
    gn = gn_ref[...]

    for c in range(ct // GLA_CHUNK):
        rows = slice(c * GLA_CHUNK, (c + 1) * GLA_CHUNK)
        b = jnp.dot(tri, gk_all[rows], preferred_element_type=F32, precision=lax.Precision.HIGHEST)
        b_last = b[GLA_CHUNK - 1:GLA_CHUNK]
        q = q_ref[rows, :] * (GLA_DK ** -0.5)
        k = k_ref[rows, :]
        v = v_ref[rows, :].astype(BF16)
        st = st_ref[...]
        o = _dot_nt((q * jnp.exp(b)).astype(BF16), st.astype(BF16))
        parts = []
        for i in range(GLA_CHUNK // GLA_SUB):
            sub = slice(i * GLA_SUB, (i + 1) * GLA_SUB)
            r = b[i * GLA_SUB:i * GLA_SUB + 1]
            q_i = (q[sub] * jnp.exp(b[sub] - r)).astype(BF16)
            k_i = (k * jnp.exp(r - b)).astype(BF16)
            parts.append(_dot_nt(q_i, k_i))
        attn = jnp.where(causal, jnp.concatenate(parts, axis=0), 0.0)
        o = o + _dot(attn.astype(BF16), v)
        k_hat = (k * jnp.exp(b_last - b)).astype(BF16)
        st_ref[...] = st * jnp.exp(b_last) + _dot_tn(v, k_hat)
        o = _rmsnorm(o, gn)
        g = g_ref[rows, :]
        o_ref[rows, :] = (o * (g * _sigmoid(g))).astype(BF16)

    @pl.when(ci == pl.num_programs(2) - 1)
    def _():
        sfin_ref[0, 0] = st_ref[...].T


def gla_prompt(z, wgk, bgk, gla_norm, batch, seq, ct):
    m = z.shape[0]
    nc = seq // ct
    row = lambda b, h, c: b * nc + c
    return pl.pallas_call(
        functools.partial(_gla_prompt_kernel, ct=ct),
        grid=(batch, GLA_HEADS, nc),
        in_specs=[
            pl.BlockSpec((ct, GLA_DK), lambda b, h, c: (row(b, h, c), Z_Q // GLA_DK + h)),
            pl.BlockSpec((ct, GLA_DK), lambda b, h, c: (row(b, h, c), Z_K // GLA_DK + h)),
            pl.BlockSpec((ct, GLA_DV), lambda b, h, c: (row(b, h, c), Z_V // GLA_DV + h)),
            pl.BlockSpec((ct, GLA_DV), lambda b, h, c: (row(b, h, c), Z_G // GLA_DV + h)),
            pl.BlockSpec((ct, TAIL_W), lambda b, h, c: (row(b, h, c), Z_TAIL // TAIL_W)),
            pl.BlockSpec((TAIL_W, GLA_DK), lambda b, h, c: (0, h)),
            pl.BlockSpec((1, GLA_DK), lambda b, h, c: (0, h)),
            pl.BlockSpec((1, GLA_DV), lambda b, h, c: (0, 0)),
        ],
        out_specs=[
            pl.BlockSpec((ct, GLA_DV), lambda b, h, c: (row(b, h, c), h)),
            pl.BlockSpec((1, 1, GLA_DK, GLA_DV), lambda b, h, c: (b, h, 0, 0)),
        ],
        out_shape=[
            jax.ShapeDtypeStruct((m, GLA_HEADS * GLA_DV), BF16),
            jax.ShapeDtypeStruct((batch, GLA_HEADS, GLA_DK, GLA_DV), F32),
        ],
        scratch_shapes=[pltpu.VMEM((GLA_DV, GLA_DK), F32)],
        compiler_params=_cparams("parallel", "parallel", "arbitrary"),
        name="gla_prompt",
    )(z, z, z, z, z, wgk, bgk.reshape(1, -1), gla_norm.reshape(1, -1))


def _gla_decode_kernel(q_ref, k_ref, v_ref, g_ref, tail_ref, wgk_ref, bgk_ref, gn_ref, s_ref,
                       o_ref, snew_ref, *, tb):
    pre = _dot(tail_ref[...].astype(BF16), wgk_ref[...]) + bgk_ref[...]
    decay = jnp.exp(_log_sigmoid(pre) * (1.0 / GLA_GATE_NORM))
    q_all = q_ref[...] * (GLA_DK ** -0.5)
    k_all = k_ref[...]
    v_all = v_ref[...]
    g_all = g_ref[...]
    gn = gn_ref[...]
    eye = (lax.broadcasted_iota(jnp.int32, (GLA_DK, GLA_DK), 0)
           == lax.broadcasted_iota(jnp.int32, (GLA_DK, GLA_DK), 1))

    def column(r):
        return jnp.sum(jnp.where(eye, r, 0.0), axis=1, keepdims=True)

    for t in range(tb):
        for h in range(GLA_HEADS):
            ks = slice(h * GLA_DK, (h + 1) * GLA_DK)
            vs = slice(h * GLA_DV, (h + 1) * GLA_DV)
            q_c = column(q_all[t:t + 1, ks])
            k_c = column(k_all[t:t + 1, ks])
            d_c = column(decay[t:t + 1, ks])
            s_new = d_c * s_ref[t, h] + k_c * v_all[t:t + 1, vs]
            snew_ref[t, h] = s_new
            o = jnp.sum(q_c * s_new, axis=0, keepdims=True)
            o = _rmsnorm(o, gn)
            g = g_all[t:t + 1, vs]
            o_ref[t:t + 1, vs] = (o * (g * _sigmoid(g))).astype(BF16)


def gla_decode(z, wgk, bgk, gla_norm, state, tb):
    m = z.shape[0]
    qk_w = GLA_HEADS * GLA_DK
    v_w = GLA_HEADS * GLA_DV
    return pl.pallas_call(
        functools.partial(_gla_decode_kernel, tb=tb),
        grid=(m // tb,),
        in_specs=[
            pl.BlockSpec((tb, qk_w), lambda i: (i, Z_Q // qk_w)),
            pl.BlockSpec((tb, qk_w), lambda i: (i, Z_K // qk_w)),
            pl.BlockSpec((tb, v_w), lambda i: (i, Z_V // v_w)),
            pl.BlockSpec((tb, v_w), lambda i: (i, Z_G // v_w)),
            pl.BlockSpec((tb, TAIL_W), lambda i: (i, Z_TAIL // TAIL_W)),
            pl.BlockSpec((TAIL_W, qk_w), lambda i: (0, 0)),
            pl.BlockSpec((1, qk_w), lambda i: (0, 0)),
            pl.BlockSpec((1, GLA_DV), lambda i: (0, 0)),
            pl.BlockSpec((tb, GLA_HEADS, GLA_DK, GLA_DV), lambda i: (i, 0, 0, 0)),
        ],
        out_specs=[
            pl.BlockSpec((tb, v_w), lambda i: (i, 0)),
            pl.BlockSpec((tb, GLA_HEADS, GLA_DK, GLA_DV), lambda i: (i, 0, 0, 0)),
        ],
        out_shape=[
            jax.ShapeDtypeStruct((m, v_w), BF16),
            jax.ShapeDtypeStruct(state.shape, F32),
        ],
        compiler_params=_cparams("parallel"),
        name="gla_decode",
    )(z, z, z, z, z, wgk, bgk.reshape(1, -1), gla_norm.reshape(1, -1), state)


def _rot_cols(w):
    half = w.shape[-1] // 2
    return jnp.concatenate([-w[..., half:], w[..., :half]], axis=-1)


def _prep_layer_weights(w_in, w_gk2, w_qb, w_kb, w_vb, w_o, w_ffn_in, w_ffn_out, w_ple_gate, w_ple_proj):
    d = w_in.shape[0]
    qk = GLA_HEADS * GLA_DK
    vw = GLA_HEADS * GLA_DV
    o_q, o_k, o_v = 0, qk, 2 * qk
    o_glr = o_v + vw
    o_g = o_glr + GLA_GATE_RANK
    o_cq = o_g + vw
    o_ckv = o_cq + MLA_Q_RANK
    o_kpe = o_ckv + MLA_KV_RANK
    w_kpe = w_in[:, o_kpe:o_kpe + MLA_ROPE]
    pad = jnp.zeros((d, TAIL_W - 2 * MLA_ROPE - GLA_GATE_RANK), F32)
    w_in_r = jnp.concatenate([
        w_in[:, o_q:o_q + qk], w_in[:, o_k:o_k + qk], w_in[:, o_v:o_v + vw], w_in[:, o_g:o_g + vw],
        w_in[:, o_cq:o_cq + MLA_Q_RANK], w_in[:, o_ckv:o_ckv + MLA_KV_RANK],
        w_kpe, _rot_cols(w_kpe), w_in[:, o_glr:o_glr + GLA_GATE_RANK], pad], axis=1).astype(BF16)
    wgk = jnp.zeros((TAIL_W, qk), F32).at[TAIL_GLR:TAIL_GLR + GLA_GATE_RANK].set(w_gk2).astype(BF16)
    wq = w_qb.reshape(MLA_Q_RANK, MLA_HEADS, MLA_NOPE + MLA_ROPE)
    wq_pe = wq[..., MLA_NOPE:]
    wqb_r = jnp.concatenate([wq[..., :MLA_NOPE], wq_pe, _rot_cols(wq_pe)], axis=-1)
    wqb_r = wqb_r.reshape(MLA_Q_RANK, MLA_HEADS * Q_HEAD_COLS).astype(BF16)
    wkb_t = jnp.transpose(w_kb, (1, 2, 0)).astype(BF16)
    wvb_h = jnp.transpose(w_vb, (1, 0, 2)).astype(BF16)
    w_o = w_o.astype(BF16)
    return dict(w_in=w_in_r, wgk=wgk, wqb=wqb_r, wkb=wkb_t, wvb=wvb_h,
                w_o_a=w_o[:vw], w_o_b=w_o[vw:], w_ffn_in=w_ffn_in.astype(BF16),
                w_ffn_out=w_ffn_out.astype(BF16), w_ple_gate=w_ple_gate.astype(BF16),
                w_ple_proj=w_ple_proj.astype(BF16))


def _rope_tables(pos):
    freqs = ROPE_THETA ** (-jnp.arange(0, MLA_ROPE, 2, dtype=F32) / MLA_ROPE)
    ang = pos[:, None] * freqs[None, :]
    cos, sin = jnp.cos(ang), jnp.sin(ang)
    return jnp.concatenate([cos, cos], axis=-1), jnp.concatenate([sin, sin], axis=-1)


def _pick(m, target):
    return min(m, target)


def _dense_tail(x, mix_a, mix_b, p, w, norm_ffn, norm_ple, tm):
    x = matmul_residual(x, [(mix_a, w["w_o_a"]), (mix_b, w["w_o_b"])], tm, 1024, "out_proj")
    act = ffn_in(x, norm_ffn, w["w_ffn_in"], tm, 512)
    x = matmul_residual(x, [(act, w["w_ffn_out"])], tm, 512, "ffn_out")
    return ple(x, norm_ple, w["w_ple_gate"], p, w["w_ple_proj"], tm, 1024)


def kernel(x_prompt, x_sample, state_gla, cache_ckv, cache_kpe, page_table, p_prompt, p_sample,
           norm_mix, w_in, w_gk2, b_gk, gla_norm, q_a_norm, kv_a_norm, w_qb, w_kb, w_vb, w_o,
           norm_ffn, w_ffn_in, w_ffn_out, norm_ple, w_ple_gate, w_ple_proj, norm_final):
    bsz, seq, d = x_prompt.shape
    nb, ls, _ = x_sample.shape
    assert ls == 1
    depth = w_in.shape[0]
    past = page_table.shape[1] * cache_ckv.shape[2]
    mp = bsz * seq
    tm_p = _pick(mp, 1024)
    tm_s = nb

    cos_p, sin_p = _rope_tables(jnp.arange(seq, dtype=F32))
    cos_s, sin_s = _rope_tables(jnp.full((nb,), float(past), F32))

    xp = x_prompt.reshape(mp, d)
    xs = x_sample.reshape(nb, d)
    gla_p, gla_s, ckv_p, kpe_p, ckv_s, kpe_s = [], [], [], [], [], []
    for i in range(depth):
        w = _prep_layer_weights(w_in[i], w_gk2[i], w_qb[i], w_kb[i], w_vb[i], w_o[i], w_ffn_in[i],
                                w_ffn_out[i], w_ple_gate[i], w_ple_proj[i])
        z = norm_matmul(xp, norm_mix[i], w["w_in"], tm_p, 1024)
        mix_a, s_fin = gla_prompt(z, w["wgk"], b_gk[i], gla_norm[i], bsz, seq, _pick(seq, 512))
        q, ckv, kpe, kvcat = mla_prep(z, cos_p, sin_p, q_a_norm[i], kv_a_norm[i], w["wqb"], w["wkb"],
                                      _pick(seq, 512))
        mix_b = mla_prompt_attention(q, kvcat, w["wvb"], bsz, seq, _pick(seq, 256))
        xp = _dense_tail(xp, mix_a, mix_b, p_prompt[i].reshape(mp, -1), w, norm_ffn[i], norm_ple[i], tm_p)
        gla_p.append(s_fin)
        ckv_p.append(ckv.reshape(bsz, seq, MLA_KV_RANK))
        kpe_p.append(kpe.reshape(bsz, seq, MLA_ROPE))
        zs = norm_matmul(xs, norm_mix[i], w["w_in"], tm_s, 1024)
        mix_a_s, s_new = gla_decode(zs, w["wgk"], b_gk[i], gla_norm[i], state_gla[i], 8)
        q_s, ckv_n, kpe_n, kvcat_n = mla_prep(zs, cos_s, sin_s, q_a_norm[i], kv_a_norm[i], w["wqb"],
                                              w["wkb"], tm_s)
        q_s = jnp.transpose(q_s, (1, 0, 2))
        mix_b_s = mla_decode_attention(page_table, q_s, kvcat_n, w["wvb"], cache_ckv, cache_kpe, i)
        xs = _dense_tail(xs, mix_a_s, mix_b_s, p_sample[i].reshape(nb, -1), w, norm_ffn[i], norm_ple[i], tm_s)
        gla_s.append(s_new)
        ckv_s.append(ckv_n.reshape(nb, ls, MLA_KV_RANK))
        kpe_s.append(kpe_n.reshape(nb, ls, MLA_ROPE))

    y_prompt = final_norm(xp, norm_final, tm_p).reshape(bsz, seq, d)
    y_sample = final_norm(xs, norm_final, tm_s).reshape(nb, ls, d)
    return (y_prompt, y_sample, jnp.stack(gla_p), jnp.stack(gla_s), jnp.stack(ckv_p),
            jnp.stack(kpe_p), jnp.stack(ckv_s), jnp.stack(kpe_s))
```

```python
import functools

import jax
import jax.numpy as jnp
import numpy as np
from jax import lax
from jax.experimental import pallas as pl
from jax.experimental.pallas import tpu as pltpu

F32 = jnp.float32
BF16 = jnp.bfloat16

GLA_HEADS = 4
GLA_DK = 128
GLA_DV = 256
GLA_GATE_RANK = 16
GLA_GATE_NORM = 16.0
MLA_HEADS = 8
MLA_Q_RANK = 512
MLA_KV_RANK = 256
MLA_NOPE = 128
MLA_ROPE = 64
MLA_DV = 128
ROPE_THETA = 10000.0
EPS = 1e-6

QK_WIDTH = MLA_KV_RANK + MLA_ROPE
QK_PAD = 384
Q_HEAD_COLS = MLA_NOPE + 2 * MLA_ROPE

Z_Q = 0
Z_K = Z_Q + GLA_HEADS * GLA_DK
Z_V = Z_K + GLA_HEADS * GLA_DK
Z_G = Z_V + GLA_HEADS * GLA_DV
Z_CQ = Z_G + GLA_HEADS * GLA_DV
Z_CKV = Z_CQ + MLA_Q_RANK
Z_TAIL = Z_CKV + MLA_KV_RANK
TAIL_W = 256
TAIL_KPE = 0
TAIL_KPE_ROT = MLA_ROPE
TAIL_GLR = 2 * MLA_ROPE
Z_WIDTH = Z_TAIL + TAIL_W

VMEM_LIMIT_BYTES = 56 * 1024 * 1024

GLA_CHUNK = 128
GLA_SUB = 16


def _cparams(*sem):
    return pltpu.CompilerParams(dimension_semantics=sem, vmem_limit_bytes=VMEM_LIMIT_BYTES)


def _rmsnorm(x, g):
    ms = jnp.mean(x * x, axis=-1, keepdims=True)
    return x * lax.rsqrt(ms + EPS) * g


def _sigmoid(x):
    return 1.0 / (1.0 + jnp.exp(-x))


def _log_sigmoid(x):
    return jnp.minimum(x, 0.0) - jnp.log1p(jnp.exp(-jnp.abs(x)))


def _dot(a, b):
    return jnp.dot(a, b, preferred_element_type=F32)


def _dot_nt(a, b):
    return lax.dot_general(a, b, (((1,), (1,)), ((), ())), preferred_element_type=F32)


def _dot_tn(a, b):
    return lax.dot_general(a, b, (((0,), (0,)), ((), ())), preferred_element_type=F32)


def _norm_mm_kernel(x_ref, g_ref, w_ref, o_ref, h_ref):
    @pl.when(pl.program_id(1) == 0)
    def _():
        h_ref[...] = _rmsnorm(x_ref[...], g_ref[...]).astype(BF16)

    o_ref[...] = _dot(h_ref[...], w_ref[...])


def norm_matmul(x, g, w, tm, tn):
    m, k = x.shape
    n = w.shape[1]
    return pl.pallas_call(
        _norm_mm_kernel,
        grid=(m // tm, n // tn),
        in_specs=[
            pl.BlockSpec((tm, k), lambda i, j: (i, 0)),
            pl.BlockSpec((1, k), lambda i, j: (0, 0)),
            pl.BlockSpec((k, tn), lambda i, j: (0, j)),
        ],
        out_specs=pl.BlockSpec((tm, tn), lambda i, j: (i, j)),
        out_shape=jax.ShapeDtypeStruct((m, n), F32),
        scratch_shapes=[pltpu.VMEM((tm, k), BF16)],
        compiler_params=_cparams("parallel", "arbitrary"),
        name="in_proj",
    )(x, g.reshape(1, k), w)


def _ffn_in_kernel(x_ref, g_ref, wg_ref, wu_ref, o_ref, h_ref):
    @pl.when(pl.program_id(1) == 0)
    def _():
        h_ref[...] = _rmsnorm(x_ref[...], g_ref[...]).astype(BF16)

    h = h_ref[...]
    a = _dot(h, wg_ref[...])
    u = _dot(h, wu_ref[...])
    o_ref[...] = (a * _sigmoid(a) * u).astype(BF16)


def ffn_in(x, g, w, tm, tn):
    m, k = x.shape
    d_ff = w.shape[1] // 2
    nj = d_ff // tn
    return pl.pallas_call(
        _ffn_in_kernel,
        grid=(m // tm, nj),
        in_specs=[
            pl.BlockSpec((tm, k), lambda i, j: (i, 0)),
            pl.BlockSpec((1, k), lambda i, j: (0, 0)),
            pl.BlockSpec((k, tn), lambda i, j: (0, j)),
            pl.BlockSpec((k, tn), lambda i, j: (0, j + nj)),
        ],
        out_specs=pl.BlockSpec((tm, tn), lambda i, j: (i, j)),
        out_shape=jax.ShapeDtypeStruct((m, d_ff), BF16),
        scratch_shapes=[pltpu.VMEM((tm, k), BF16)],
        compiler_params=_cparams("parallel", "arbitrary"),
        name="ffn_in",
    )(x, g.reshape(1, k), w, w)


def _mm_res_kernel(*refs):
    res_ref, o_ref = refs[0], refs[-1]
    acc = res_ref[...]
    for a_ref, w_ref in zip(refs[1:-1:2], refs[2:-1:2]):
        acc = acc + _dot(a_ref[...], w_ref[...])
    o_ref[...] = acc


def matmul_residual(res, pairs, tm, tn, name):
    m, n = res.shape
    in_specs = [pl.BlockSpec((tm, tn), lambda i, j: (i, j))]
    args = [res]
    for a, w in pairs:
        k = a.shape[1]
        in_specs.append(pl.BlockSpec((tm, k), lambda i, j: (i, 0)))
        in_specs.append(pl.BlockSpec((k, tn), lambda i, j: (0, j)))
        args += [a, w]
    return pl.pallas_call(
        _mm_res_kernel,
        grid=(m // tm, n // tn),
        in_specs=in_specs,
        out_specs=pl.BlockSpec((tm, tn), lambda i, j: (i, j)),
        out_shape=jax.ShapeDtypeStruct((m, n), F32),
        compiler_params=_cparams("parallel", "arbitrary"),
        name=name,
    )(*args)


def _ple_kernel(x_ref, g_ref, wg_ref, p_ref, wp_ref, xres_ref, o_ref, h_ref):
    @pl.when(pl.program_id(1) == 0)
    def _():
        h_ref[...] = _rmsnorm(x_ref[...], g_ref[...]).astype(BF16)

    gate = _sigmoid(_dot(h_ref[...], wg_ref[...]))
    emb = _dot(p_ref[...].astype(BF16), wp_ref[...])
    o_ref[...] = xres_ref[...] + gate * emb


def ple(x, g, wg, p, wp, tm, tn):
    m, k = x.shape
    n = wg.shape[1]
    kp = p.shape[1]
    return pl.pallas_call(
        _ple_kernel,
        grid=(m // tm, n // tn),
        in_specs=[
            pl.BlockSpec((tm, k), lambda i, j: (i, 0)),
            pl.BlockSpec((1, k), lambda i, j: (0, 0)),
            pl.BlockSpec((k, tn), lambda i, j: (0, j)),
            pl.BlockSpec((tm, kp), lambda i, j: (i, 0)),
            pl.BlockSpec((kp, tn), lambda i, j: (0, j)),
            pl.BlockSpec((tm, tn), lambda i, j: (i, j)),
        ],
        out_specs=pl.BlockSpec((tm, tn), lambda i, j: (i, j)),
        out_shape=jax.ShapeDtypeStruct((m, n), F32),
        scratch_shapes=[pltpu.VMEM((tm, k), BF16)],
        compiler_params=_cparams("parallel", "arbitrary"),
        name="ple",
    )(x, g.reshape(1, k), wg, p, wp, x)


def _final_norm_kernel(x_ref, g_ref, o_ref):
    o_ref[...] = _rmsnorm(x_ref[...], g_ref[...])


def final_norm(x, g, tm):
    m, k = x.shape
    return pl.pallas_call(
        _final_norm_kernel,
        grid=(m // tm,),
        in_specs=[pl.BlockSpec((tm, k), lambda i: (i, 0)), pl.BlockSpec((1, k), lambda i: (0, 0))],
        out_specs=pl.BlockSpec((tm, k), lambda i: (i, 0)),
        out_shape=jax.ShapeDtypeStruct((m, k), F32),
        compiler_params=_cparams("parallel"),
        name="final_norm",
    )(x, g.reshape(1, k))


def _mla_prep_kernel(cq_ref, ckv_ref, tail_ref, cos_ref, sin_ref, qn_ref, kvn_ref, wqb_ref, wkb_ref,
                     q_ref, ckv_out_ref, kpe_out_ref, kvcat_ref):
    scale = (MLA_NOPE + MLA_ROPE) ** -0.5
    cos = cos_ref[...]
    sin = sin_ref[...]
    cqn = _rmsnorm(cq_ref[...], qn_ref[...]).astype(BF16)
    qb = _dot(cqn, wqb_ref[...])
    for h in range(MLA_HEADS):
        base = h * Q_HEAD_COLS
        nope = qb[:, base:base + MLA_NOPE]
        pe = qb[:, base + MLA_NOPE:base + MLA_NOPE + MLA_ROPE]
        pe_rot = qb[:, base + MLA_NOPE + MLA_ROPE:base + Q_HEAD_COLS]
        q_lat = _dot(nope.astype(BF16), wkb_ref[h])
        q_pe = pe * cos + pe_rot * sin
        q_ref[h, :, 0:MLA_KV_RANK] = (q_lat * scale).astype(BF16)
        q_ref[h, :, MLA_KV_RANK:QK_WIDTH] = (q_pe * scale).astype(BF16)
        q_ref[h, :, QK_WIDTH:QK_PAD] = jnp.zeros((q_pe.shape[0], QK_PAD - QK_WIDTH), BF16)
    ckv = _rmsnorm(ckv_ref[...], kvn_ref[...])
    tail = tail_ref[...]
    kpe = (tail[:, TAIL_KPE:TAIL_KPE + MLA_ROPE] * cos
           + tail[:, TAIL_KPE_ROT:TAIL_KPE_ROT + MLA_ROPE] * sin)
    ckv_out_ref[...] = ckv
    kpe_out_ref[...] = kpe
    kvcat_ref[:, 0:MLA_KV_RANK] = ckv.astype(BF16)
    kvcat_ref[:, MLA_KV_RANK:QK_WIDTH] = kpe.astype(BF16)
    kvcat_ref[:, QK_WIDTH:QK_PAD] = jnp.zeros((kpe.shape[0], QK_PAD - QK_WIDTH), BF16)


def mla_prep(z, cos, sin, q_a_norm, kv_a_norm, wqb, wkb_t, tm):
    m = z.shape[0]
    n_pos_blocks = cos.shape[0] // tm
    return pl.pallas_call(
        _mla_prep_kernel,
        grid=(m // tm,),
        in_specs=[
            pl.BlockSpec((tm, MLA_Q_RANK), lambda i: (i, Z_CQ // MLA_Q_RANK)),
            pl.BlockSpec((tm, MLA_KV_RANK), lambda i: (i, Z_CKV // MLA_KV_RANK)),
            pl.BlockSpec((tm, TAIL_W), lambda i: (i, Z_TAIL // TAIL_W)),
            pl.BlockSpec((tm, MLA_ROPE), lambda i: (i % n_pos_blocks, 0)),
            pl.BlockSpec((tm, MLA_ROPE), lambda i: (i % n_pos_blocks, 0)),
            pl.BlockSpec((1, MLA_Q_RANK), lambda i: (0, 0)),
            pl.BlockSpec((1, MLA_KV_RANK), lambda i: (0, 0)),
            pl.BlockSpec(wqb.shape, lambda i: (0, 0)),
            pl.BlockSpec(wkb_t.shape, lambda i: (0, 0, 0)),
        ],
        out_specs=[
            pl.BlockSpec((MLA_HEADS, tm, QK_PAD), lambda i: (0, i, 0)),
            pl.BlockSpec((tm, MLA_KV_RANK), lambda i: (i, 0)),
            pl.BlockSpec((tm, MLA_ROPE), lambda i: (i, 0)),
            pl.BlockSpec((tm, QK_PAD), lambda i: (i, 0)),
        ],
        out_shape=[
            jax.ShapeDtypeStruct((MLA_HEADS, m, QK_PAD), BF16),
            jax.ShapeDtypeStruct((m, MLA_KV_RANK), F32),
            jax.ShapeDtypeStruct((m, MLA_ROPE), F32),
            jax.ShapeDtypeStruct((m, QK_PAD), BF16),
        ],
        compiler_params=_cparams("parallel"),
        name="mla_prep",
    )(z, z, z, cos, sin, q_a_norm.reshape(1, -1), kv_a_norm.reshape(1, -1), wqb, wkb_t)


def _mla_prompt_kernel(q_ref, kv_ref, wvb_ref, o_ref, olat_ref, *, tq, nq):
    qi = pl.program_id(1)
    t = lax.broadcasted_iota(jnp.int32, (tq, tq), 0)
    c = lax.broadcasted_iota(jnp.int32, (tq, tq), 1)
    causal = c <= t

    for v in range(nq):
        n_past = v * tq

        @pl.when(qi == v)
        def _(n_past=n_past):
            def head(h, carry):
                q = q_ref[h]
                kv_d = kv_ref[n_past:n_past + tq, :]
                s_d = jnp.where(causal, _dot_nt(q, kv_d), -jnp.inf)
                m = jnp.max(s_d, axis=-1, keepdims=True)
                if n_past:
                    s_p = _dot_nt(q, kv_ref[0:n_past, :])
                    m = jnp.maximum(m, jnp.max(s_p, axis=-1, keepdims=True))
                p_d = jnp.exp(s_d - m)
                l = jnp.sum(p_d, axis=-1, keepdims=True)
                o = _dot(p_d.astype(BF16), kv_d[:, 0:MLA_KV_RANK])
                if n_past:
                    p_p = jnp.exp(s_p - m)
                    l = l + jnp.sum(p_p, axis=-1, keepdims=True)
                    o = o + _dot(p_p.astype(BF16), kv_ref[0:n_past, 0:MLA_KV_RANK])
                olat_ref[h] = o / l
                return carry

            lax.fori_loop(0, MLA_HEADS, head, 0, unroll=4)

    for h in range(MLA_HEADS):
        o_ref[:, h * MLA_DV:(h + 1) * MLA_DV] = _dot(olat_ref[h].astype(BF16), wvb_ref[h]).astype(BF16)


def mla_prompt_attention(q, kvcat, wvb, batch, seq, tq):
    nq = seq // tq
    return pl.pallas_call(
        functools.partial(_mla_prompt_kernel, tq=tq, nq=nq),
        grid=(batch, nq),
        in_specs=[
            pl.BlockSpec((MLA_HEADS, tq, QK_PAD), lambda b, i: (0, b * nq + i, 0)),
            pl.BlockSpec((seq, QK_PAD), lambda b, i: (b, 0)),
            pl.BlockSpec(wvb.shape, lambda b, i: (0, 0, 0)),
        ],
        out_specs=pl.BlockSpec((tq, MLA_HEADS * MLA_DV), lambda b, i: (b * nq + i, 0)),
        out_shape=jax.ShapeDtypeStruct((batch * seq, MLA_HEADS * MLA_DV), BF16),
        scratch_shapes=[pltpu.VMEM((MLA_HEADS, tq, MLA_KV_RANK), F32)],
        compiler_params=_cparams("parallel", "arbitrary"),
        name="mla_prompt_attn",
    )(q, kvcat, wvb)


ROWS_PER_STEP = 2


def _mla_decode_kernel(pt_ref, q_ref, kvn_ref, wvb_ref, ckv_hbm, kpet_hbm, o_ref,
                       ckv_buf, kpet_buf, sem, olat_ref, *, layer, page, n_pages):
    i = pl.program_id(0)
    n = pl.num_programs(0)

    def copies(row, slot):
        out = []
        for j in range(n_pages):
            pg = pt_ref[row, j]
            keys = pl.ds(j * page, page)
            out.append(pltpu.make_async_copy(ckv_hbm.at[layer, pg], ckv_buf.at[slot, keys], sem.at[slot]))
            out.append(pltpu.make_async_copy(kpet_hbm.at[layer, pg], kpet_buf.at[slot, :, keys], sem.at[slot]))
        return out

    def attend(r, slot):
        q = q_ref[r]
        kvn = kvn_ref[r]
        ckv = ckv_buf[slot].astype(BF16)
        kpet = kpet_buf[slot].astype(BF16)
        s = _dot_nt(q[:, 0:MLA_KV_RANK], ckv) + _dot(q[:, MLA_KV_RANK:QK_WIDTH], kpet)
        s_new = jnp.sum(q.astype(F32) * kvn.astype(F32), axis=-1, keepdims=True)
        m = jnp.maximum(jnp.max(s, axis=-1, keepdims=True), s_new)
        p = jnp.exp(s - m)
        p_new = jnp.exp(s_new - m)
        l = jnp.sum(p, axis=-1, keepdims=True) + p_new
        acc = _dot(p.astype(BF16), ckv) + p_new * kvn[:, 0:MLA_KV_RANK].astype(F32)
        o_lat = acc / l
        row = i * ROWS_PER_STEP + r
        for h in range(MLA_HEADS):
            olat_ref[h, pl.ds(row, 1), :] = o_lat[h:h + 1]

    @pl.when(i == 0)
    def _():
        for cp in copies(0, 0):
            cp.start()

    for r in range(ROWS_PER_STEP):
        row = i * ROWS_PER_STEP + r
        if r + 1 < ROWS_PER_STEP:
            for cp in copies(row + 1, r + 1):
                cp.start()
        else:
            @pl.when(i + 1 < n)
            def _():
                for cp in copies(row + 1, 0):
                    cp.start()
        for cp in copies(row, r):
            cp.wait()
        attend(r, r)

    @pl.when(i == n - 1)
    def _():
        for h in range(MLA_HEADS):
            o_ref[:, h * MLA_DV:(h + 1) * MLA_DV] = _dot(olat_ref[h].astype(BF16), wvb_ref[h]).astype(BF16)


def mla_decode_attention(page_table, q, kvcat_new, wvb, cache_ckv, cache_kpe_t, layer):
    nb, n_pages = page_table.shape
    page = cache_ckv.shape[2]
    assert nb % ROWS_PER_STEP == 0
    keys = n_pages * page
    grid_spec = pltpu.PrefetchScalarGridSpec(
        num_scalar_prefetch=1,
        grid=(nb // ROWS_PER_STEP,),
        in_specs=[
            pl.BlockSpec((ROWS_PER_STEP, MLA_HEADS, QK_PAD), lambda i, pt: (i, 0, 0)),
            pl.BlockSpec((ROWS_PER_STEP, 1, QK_PAD), lambda i, pt: (i, 0, 0)),
            pl.BlockSpec(wvb.shape, lambda i, pt: (0, 0, 0)),
            pl.BlockSpec(memory_space=pl.ANY),
            pl.BlockSpec(memory_space=pl.ANY),
        ],
        out_specs=pl.BlockSpec((nb, MLA_HEADS * MLA_DV), lambda i, pt: (0, 0)),
        scratch_shapes=[
            pltpu.VMEM((ROWS_PER_STEP, keys, MLA_KV_RANK), F32),
            pltpu.VMEM((ROWS_PER_STEP, MLA_ROPE, keys), F32),
            pltpu.SemaphoreType.DMA((ROWS_PER_STEP,)),
            pltpu.VMEM((MLA_HEADS, nb, MLA_KV_RANK), F32),
        ],
    )
    return pl.pallas_call(
        functools.partial(_mla_decode_kernel, layer=layer, page=page, n_pages=n_pages),
        grid_spec=grid_spec,
        out_shape=jax.ShapeDtypeStruct((nb, MLA_HEADS * MLA_DV), BF16),
        compiler_params=_cparams("arbitrary"),
        name="mla_decode_attn",
    )(page_table, q, kvcat_new.reshape(nb, 1, QK_PAD), wvb, cache_ckv, cache_kpe_t)


def _gla_prompt_kernel(q_ref, k_ref, v_ref, g_ref, tail_ref, wgk_ref, bgk_ref, gn_ref,
                       o_ref, sfin_ref, st_ref, *, ct):
    ci = pl.program_id(2)

    @pl.when(ci == 0)
    def _():
        st_ref[...] = jnp.zeros_like(st_ref)

    pre = _dot(tail_ref[...].astype(BF16), wgk_ref[...]) + bgk_ref[...]
    gk_all = _log_sigmoid(pre) * (1.0 / GLA_GATE_NORM)
    row = lax.broadcasted_iota(jnp.int32, (GLA_CHUNK, GLA_CHUNK), 0)
    col = lax.broadcasted_iota(jnp.int32, (GLA_CHUNK, GLA_CHUNK), 1)
    causal = col <= row
    tri = causal.astype(F32)
    gn = gn_ref[...]

    for c in range(ct // GLA_CHUNK):
        rows = slice(c * GLA_CHUNK, (c + 1) * GLA_CHUNK)
        b = jnp.dot(tri, gk_all[rows], preferred_element_type=F32, precision=lax.Precision.HIGHEST)
        b_last = b[GLA_CHUNK - 1:GLA_CHUNK]
        q = q_ref[rows, :] * (GLA_DK ** -0.5)
        k = k_ref[rows, :]
        v = v_ref[rows, :].astype(BF16)
        st = st_ref[...]
        o = _dot_nt((q * jnp.exp(b)).astype(BF16), st.astype(BF16))
        parts = []
        for i in range(GLA_CHUNK // GLA_SUB):
            sub = slice(i * GLA_SUB, (i + 1) * GLA_SUB)
            r = b[i * GLA_SUB:i * GLA_SUB + 1]
            q_i = (q[sub] * jnp.exp(b[sub] - r)).astype(BF16)
            k_i = (k * jnp.exp(r - b)).astype(BF16)
            parts.append(_dot_nt(q_i, k_i))
        attn = jnp.where(causal, jnp.concatenate(parts, axis=0), 0.0)
        o = o + _dot(attn.astype(BF16), v)
        k_hat = (k * jnp.exp(b_last - b)).astype(BF16)
        st_ref[...] = st * jnp.exp(b_last) + _dot_tn(v, k_hat)
        o = _rmsnorm(o, gn)
        g = g_ref[rows, :]
        o_ref[rows, :] = (o * (g * _sigmoid(g))).astype(BF16)

    @pl.when(ci == pl.num_programs(2) - 1)
    def _():
        sfin_ref[0, 0] = st_ref[...].T


def gla_prompt(z, wgk, bgk, gla_norm, batch, seq, ct):
    m = z.shape[0]
    nc = seq // ct
    row = lambda b, h, c: b * nc + c
    return pl.pallas_call(
        functools.partial(_gla_prompt_kernel, ct=ct),
        grid=(batch, GLA_HEADS, nc),
        in_specs=[
            pl.BlockSpec((ct, GLA_DK), lambda b, h, c: (row(b, h, c), Z_Q // GLA_DK + h)),
            pl.BlockSpec((ct, GLA_DK), lambda b, h, c: (row(b, h, c), Z_K // GLA_DK + h)),
            pl.BlockSpec((ct, GLA_DV), lambda b, h, c: (row(b, h, c), Z_V // GLA_DV + h)),
            pl.BlockSpec((ct, GLA_DV), lambda b, h, c: (row(b, h, c), Z_G // GLA_DV + h)),
            pl.BlockSpec((ct, TAIL_W), lambda b, h, c: (row(b, h, c), Z_TAIL // TAIL_W)),
            pl.BlockSpec((TAIL_W, GLA_DK), lambda b, h, c: (0, h)),
            pl.BlockSpec((1, GLA_DK), lambda b, h, c: (0, h)),
            pl.BlockSpec((1, GLA_DV), lambda b, h, c: (0, 0)),
        ],
        out_specs=[
            pl.BlockSpec((ct, GLA_DV), lambda b, h, c: (row(b, h, c), h)),
            pl.BlockSpec((1, 1, GLA_DK, GLA_DV), lambda b, h, c: (b, h, 0, 0)),
        ],
        out_shape=[
            jax.ShapeDtypeStruct((m, GLA_HEADS * GLA_DV), BF16),
            jax.ShapeDtypeStruct((batch, GLA_HEADS, GLA_DK, GLA_DV), F32),
        ],
        scratch_shapes=[pltpu.VMEM((GLA_DV, GLA_DK), F32)],
        compiler_params=_cparams("parallel", "parallel", "arbitrary"),
        name="gla_prompt",
    )(z, z, z, z, z, wgk, bgk.reshape(1, -1), gla_norm.reshape(1, -1))


def _gla_decode_kernel(q_ref, k_ref, v_ref, g_ref, tail_ref, wgk_ref, bgk_ref, gn_ref, s_ref,
                       o_ref, snew_ref, *, tb):
    pre = _dot(tail_ref[...].astype(BF16), wgk_ref[...]) + bgk_ref[...]
    decay = jnp.exp(_log_sigmoid(pre) * (1.0 / GLA_GATE_NORM))
    q_all = q_ref[...] * (GLA_DK ** -0.5)
    k_all = k_ref[...]
    v_all = v_ref[...]
    g_all = g_ref[...]
    gn = gn_ref[...]
    eye = (lax.broadcasted_iota(jnp.int32, (GLA_DK, GLA_DK), 0)
           == lax.broadcasted_iota(jnp.int32, (GLA_DK, GLA_DK), 1))

    def column(r):
        return jnp.sum(jnp.where(eye, r, 0.0), axis=1, keepdims=True)

    for t in range(tb):
        for h in range(GLA_HEADS):
            ks = slice(h * GLA_DK, (h + 1) * GLA_DK)
            vs = slice(h * GLA_DV, (h + 1) * GLA_DV)
            q_c = column(q_all[t:t + 1, ks])
            k_c = column(k_all[t:t + 1, ks])
            d_c = column(decay[t:t + 1, ks])
            s_new = d_c * s_ref[t, h] + k_c * v_all[t:t + 1, vs]
            snew_ref[t, h] = s_new
            o = jnp.sum(q_c * s_new, axis=0, keepdims=True)
            o = _rmsnorm(o, gn)
            g = g_all[t:t + 1, vs]
            o_ref[t:t + 1, vs] = (o * (g * _sigmoid(g))).astype(BF16)


def gla_decode(z, wgk, bgk, gla_norm, state, tb):
    m = z.shape[0]
    qk_w = GLA_HEADS * GLA_DK
    v_w = GLA_HEADS * GLA_DV
    return pl.pallas_call(
        functools.partial(_gla_decode_kernel, tb=tb),
        grid=(m // tb,),
        in_specs=[
            pl.BlockSpec((tb, qk_w), lambda i: (i, Z_Q // qk_w)),
            pl.BlockSpec((tb, qk_w), lambda i: (i, Z_K // qk_w)),
            pl.BlockSpec((tb, v_w), lambda i: (i, Z_V // v_w)),
            pl.BlockSpec((tb, v_w), lambda i: (i, Z_G // v_w)),
            pl.BlockSpec((tb, TAIL_W), lambda i: (i, Z_TAIL // TAIL_W)),
            pl.BlockSpec((TAIL_W, qk_w), lambda i: (0, 0)),
            pl.BlockSpec((1, qk_w), lambda i: (0, 0)),
            pl.BlockSpec((1, GLA_DV), lambda i: (0, 0)),
            pl.BlockSpec((tb, GLA_HEADS, GLA_DK, GLA_DV), lambda i: (i, 0, 0, 0)),
        ],
        out_specs=[
            pl.BlockSpec((tb, v_w), lambda i: (i, 0)),
            pl.BlockSpec((tb, GLA_HEADS, GLA_DK, GLA_DV), lambda i: (i, 0, 0, 0)),
        ],
        out_shape=[
            jax.ShapeDtypeStruct((m, v_w), BF16),
            jax.ShapeDtypeStruct(state.shape, F32),
        ],
        compiler_params=_cparams("parallel"),
        name="gla_decode",
    )(z, z, z, z, z, wgk, bgk.reshape(1, -1), gla_norm.reshape(1, -1), state)


def _rot_cols(w):
    half = w.shape[-1] // 2
    return jnp.concatenate([-w[..., half:], w[..., :half]], axis=-1)


def _prep_layer_weights(w_in, w_gk2, w_qb, w_kb, w_vb, w_o, w_ffn_in, w_ffn_out, w_ple_gate, w_ple_proj):
    d = w_in.shape[0]
    qk = GLA_HEADS * GLA_DK
    vw = GLA_HEADS * GLA_DV
    o_q, o_k, o_v = 0, qk, 2 * qk
    o_glr = o_v + vw
    o_g = o_glr + GLA_GATE_RANK
    o_cq = o_g + vw
    o_ckv = o_cq + MLA_Q_RANK
    o_kpe = o_ckv + MLA_KV_RANK
    w_kpe = w_in[:, o_kpe:o_kpe + MLA_ROPE]
    pad = jnp.zeros((d, TAIL_W - 2 * MLA_ROPE - GLA_GATE_RANK), F32)
    w_in_r = jnp.concatenate([
        w_in[:, o_q:o_q + qk], w_in[:, o_k:o_k + qk], w_in[:, o_v:o_v + vw], w_in[:, o_g:o_g + vw],
        w_in[:, o_cq:o_cq + MLA_Q_RANK], w_in[:, o_ckv:o_ckv + MLA_KV_RANK],
        w_kpe, _rot_cols(w_kpe), w_in[:, o_glr:o_glr + GLA_GATE_RANK], pad], axis=1).astype(BF16)
    wgk = jnp.zeros((TAIL_W, qk), F32).at[TAIL_GLR:TAIL_GLR + GLA_GATE_RANK].set(w_gk2).astype(BF16)
    wq = w_qb.reshape(MLA_Q_RANK, MLA_HEADS, MLA_NOPE + MLA_ROPE)
    wq_pe = wq[..., MLA_NOPE:]
    wqb_r = jnp.concatenate([wq[..., :MLA_NOPE], wq_pe, _rot_cols(wq_pe)], axis=-1)
    wqb_r = wqb_r.reshape(MLA_Q_RANK, MLA_HEADS * Q_HEAD_COLS).astype(BF16)
    wkb_t = jnp.transpose(w_kb, (1, 2, 0)).astype(BF16)
    wvb_h = jnp.transpose(w_vb, (1, 0, 2)).astype(BF16)
    w_o = w_o.astype(BF16)
    return dict(w_in=w_in_r, wgk=wgk, wqb=wqb_r, wkb=wkb_t, wvb=wvb_h,
                w_o_a=w_o[:vw], w_o_b=w_o[vw:], w_ffn_in=w_ffn_in.astype(BF16),
                w_ffn_out=w_ffn_out.astype(BF16), w_ple_gate=w_ple_gate.astype(BF16),
                w_ple_proj=w_ple_proj.astype(BF16))


def _rope_tables(pos):
    freqs = ROPE_THETA ** (-jnp.arange(0, MLA_ROPE, 2, dtype=F32) / MLA_ROPE)
    ang = pos[:, None] * freqs[None, :]
    cos, sin = jnp.cos(ang), jnp.sin(ang)
    return jnp.concatenate([cos, cos], axis=-1), jnp.concatenate([sin, sin], axis=-1)


def _pick(m, target):
    return min(m, target)


def _dense_tail(x, mix_a, mix_b, p, w, norm_ffn, norm_ple, tm):
    x = matmul_residual(x, [(mix_a, w["w_o_a"]), (mix_b, w["w_o_b"])], tm, 1024, "out_proj")
    act = ffn_in(x, norm_ffn, w["w_ffn_in"], tm, 512)
    x = matmul_residual(x, [(act, w["w_ffn_out"])], tm, 512, "ffn_out")
    return ple(x, norm_ple, w["w_ple_gate"], p, w["w_ple_proj"], tm, 1024)


def kernel(x_prompt, x_sample, state_gla, cache_ckv, cache_kpe, page_table, p_prompt, p_sample,
           norm_mix, w_in, w_gk2, b_gk, gla_norm, q_a_norm, kv_a_norm, w_qb, w_kb, w_vb, w_o,
           norm_ffn, w_ffn_in, w_ffn_out, norm_ple, w_ple_gate, w_ple_proj, norm_final):
    bsz, seq, d = x_prompt.shape
    nb, ls, _ = x_sample.shape
    assert ls == 1
    depth = w_in.shape[0]
    past = page_table.shape[1] * cache_ckv.shape[2]
    mp = bsz * seq
    tm_p = _pick(mp, 1024)
    tm_s = nb

    cache_kpe_t = jnp.swapaxes(cache_kpe, 2, 3)
    cos_p, sin_p = _rope_tables(jnp.arange(seq, dtype=F32))
    cos_s, sin_s = _rope_tables(jnp.full((nb,), float(past), F32))

    xp = x_prompt.reshape(mp, d)
    xs = x_sample.reshape(nb, d)
    gla_p, gla_s, ckv_p, kpe_p, ckv_s, kpe_s = [], [], [], [], [], []
    for i in range(depth):
        w = _prep_layer_weights(w_in[i], w_gk2[i], w_qb[i], w_kb[i], w_vb[i], w_o[i], w_ffn_in[i],
                                w_ffn_out[i], w_ple_gate[i], w_ple_proj[i])
        z = norm_matmul(xp, norm_mix[i], w["w_in"], tm_p, 1024)
        mix_a, s_fin = gla_prompt(z, w["wgk"], b_gk[i], gla_norm[i], bsz, seq, _pick(seq, 512))
        q, ckv, kpe, kvcat = mla_prep(z, cos_p, sin_p, q_a_norm[i], kv_a_norm[i], w["wqb"], w["wkb"],
                                      _pick(seq, 512))
        mix_b = mla_prompt_attention(q, kvcat, w["wvb"], bsz, seq, _pick(seq, 256))
        xp = _dense_tail(xp, mix_a, mix_b, p_prompt[i].reshape(mp, -1), w, norm_ffn[i], norm_ple[i], tm_p)
        gla_p.append(s_fin)
        ckv_p.append(ckv.reshape(bsz, seq, MLA_KV_RANK))
        kpe_p.append(kpe.reshape(bsz, seq, MLA_ROPE))
        zs = norm_matmul(xs, norm_mix[i], w["w_in"], tm_s, 1024)
        mix_a_s, s_new = gla_decode(zs, w["wgk"], b_gk[i], gla_norm[i], state_gla[i], 8)
        q_s, ckv_n, kpe_n, kvcat_n = mla_prep(zs, cos_s, sin_s, q_a_norm[i], kv_a_norm[i], w["wqb"],
                                              w["wkb"], tm_s)
        q_s = jnp.transpose(q_s, (1, 0, 2))
        mix_b_s = mla_decode_attention(page_table, q_s, kvcat_n, w["wvb"], cache_ckv, cache_kpe_t, i)
        xs = _dense_tail(xs, mix_a_s, mix_b_s, p_sample[i].reshape(nb, -1), w, norm_ffn[i], norm_ple[i], tm_s)
        gla_s.append(s_new)
        ckv_s.append(ckv_n.reshape(nb, ls, MLA_KV_RANK))
        kpe_s.append(kpe_n.reshape(nb, ls, MLA_ROPE))

    y_prompt = final_norm(xp, norm_final, tm_p).reshape(bsz, seq, d)
    y_sample = final_norm(xs, norm_final, tm_s).reshape(nb, ls, d)
    return (y_prompt, y_sample, jnp.stack(gla_p), jnp.stack(gla_s), jnp.stack(ckv_p),
            jnp.stack(kpe_p), jnp.stack(ckv_s), jnp.stack(kpe_s))
```

```python
import functools

import jax
import jax.numpy as jnp
import numpy as np
from jax import lax
from jax.experimental import pallas as pl
from jax.experimental.pallas import tpu as pltpu

F32 = jnp.float32
BF16 = jnp.bfloat16

GLA_HEADS = 4
GLA_DK = 128
GLA_DV = 256
GLA_GATE_RANK = 16
GLA_GATE_NORM = 16.0
MLA_HEADS = 8
MLA_Q_RANK = 512
MLA_KV_RANK = 256
MLA_NOPE = 128
MLA_ROPE = 64
MLA_DV = 128
ROPE_THETA = 10000.0
EPS = 1e-6

QK_WIDTH = MLA_KV_RANK + MLA_ROPE
QK_PAD = 384
Q_HEAD_COLS = MLA_NOPE + 2 * MLA_ROPE

Z_Q = 0
Z_K = Z_Q + GLA_HEADS * GLA_DK
Z_V = Z_K + GLA_HEADS * GLA_DK
Z_G = Z_V + GLA_HEADS * GLA_DV
Z_CQ = Z_G + GLA_HEADS * GLA_DV
Z_CKV = Z_CQ + MLA_Q_RANK
Z_TAIL = Z_CKV + MLA_KV_RANK
TAIL_W = 256
TAIL_KPE = 0
TAIL_KPE_ROT = MLA_ROPE
TAIL_GLR = 2 * MLA_ROPE
Z_WIDTH = Z_TAIL + TAIL_W

VMEM_LIMIT_BYTES = 56 * 1024 * 1024

GLA_CHUNK = 128
GLA_SUB = 16
GLA_HEADS_PER_STEP = 2


def _cparams(*sem):
    return pltpu.CompilerParams(dimension_semantics=sem, vmem_limit_bytes=VMEM_LIMIT_BYTES)


def _rmsnorm(x, g):
    ms = jnp.mean(x * x, axis=-1, keepdims=True)
    return x * lax.rsqrt(ms + EPS) * g


def _sigmoid(x):
    return 1.0 / (1.0 + jnp.exp(-x))


def _log_sigmoid(x):
    return jnp.minimum(x, 0.0) - jnp.log1p(jnp.exp(-jnp.abs(x)))


def _dot(a, b):
    return jnp.dot(a, b, preferred_element_type=F32)


def _dot_nt(a, b):
    return lax.dot_general(a, b, (((1,), (1,)), ((), ())), preferred_element_type=F32)


def _dot_tn(a, b):
    return lax.dot_general(a, b, (((0,), (0,)), ((), ())), preferred_element_type=F32)


def _norm_mm_kernel(x_ref, g_ref, w_ref, o_ref, h_ref):
    @pl.when(pl.program_id(1) == 0)
    def _():
        h_ref[...] = _rmsnorm(x_ref[...], g_ref[...]).astype(BF16)

    o_ref[...] = _dot(h_ref[...], w_ref[...])


def norm_matmul(x, g, w, tm, tn):
    m, k = x.shape
    n = w.shape[1]
    return pl.pallas_call(
        _norm_mm_kernel,
        grid=(m // tm, n // tn),
        in_specs=[
            pl.BlockSpec((tm, k), lambda i, j: (i, 0)),
            pl.BlockSpec((1, k), lambda i, j: (0, 0)),
            pl.BlockSpec((k, tn), lambda i, j: (0, j)),
        ],
        out_specs=pl.BlockSpec((tm, tn), lambda i, j: (i, j)),
        out_shape=jax.ShapeDtypeStruct((m, n), F32),
        scratch_shapes=[pltpu.VMEM((tm, k), BF16)],
        compiler_params=_cparams("parallel", "arbitrary"),
        name="in_proj",
    )(x, g.reshape(1, k), w)


STAGE_ROWS = 256


def _stage_cast(w_hbm, stage_ref, sem, dst_ref):
    n = dst_ref.shape[0] // STAGE_ROWS

    def copy(c):
        return pltpu.make_async_copy(w_hbm.at[pl.ds(c * STAGE_ROWS, STAGE_ROWS)], stage_ref.at[c % 2], sem.at[c % 2])

    copy(0).start()
    for c in range(n):
        if c + 1 < n:
            copy(c + 1).start()
        copy(c).wait()
        dst_ref[c * STAGE_ROWS:(c + 1) * STAGE_ROWS, :] = stage_ref[c % 2].astype(BF16)


def _resident_call(body, row_args, sample_args, consts, w_stack, layer, outs, tm, name):
    m = row_args[0].shape[0]
    ms = sample_args[0].shape[0]
    n_row, n_const, n_out = len(row_args), len(consts), len(outs)
    wk, wn = w_stack.shape[1:]

    def kern(*refs):
        rows = refs[0:n_row]
        srows = refs[n_row:2 * n_row]
        cs = refs[2 * n_row:2 * n_row + n_const]
        w_hbm = refs[2 * n_row + n_const]
        o = refs[2 * n_row + n_const + 1:][:n_out]
        so = refs[2 * n_row + n_const + 1 + n_out:][:n_out]
        w_ref, stage_ref, sem = refs[-3:]
        i = pl.program_id(0)

        @pl.when(i == 0)
        def _():
            _stage_cast(w_hbm.at[layer], stage_ref, sem, w_ref)

        for r, val in zip(o, body(w_ref, *[c[...] for c in cs], *[r[...] for r in rows])):
            r[...] = val.astype(r.dtype)

        @pl.when(i == pl.num_programs(0) - 1)
        def _():
            for r, val in zip(so, body(w_ref, *[c[...] for c in cs], *[r[...] for r in srows])):
                r[...] = val.astype(r.dtype)

    in_specs = ([pl.BlockSpec((tm, a.shape[1]), lambda i: (i, 0)) for a in row_args]
                + [pl.BlockSpec(a.shape, lambda i: (0, 0)) for a in sample_args]
                + [pl.BlockSpec(c.shape, lambda i, nd=c.ndim: (0,) * nd) for c in consts]
                + [pl.BlockSpec(memory_space=pl.ANY)])
    out_specs = ([pl.BlockSpec((tm, w), lambda i: (i, 0)) for w, _ in outs]
                 + [pl.BlockSpec((ms, w), lambda i: (0, 0)) for w, _ in outs])
    out_shape = ([jax.ShapeDtypeStruct((m, w), dt) for w, dt in outs]
                 + [jax.ShapeDtypeStruct((ms, w), dt) for w, dt in outs])
    res = pl.pallas_call(
        kern,
        grid=(m // tm,),
        in_specs=in_specs,
        out_specs=out_specs,
        out_shape=out_shape,
        scratch_shapes=[pltpu.VMEM((wk, wn), BF16), pltpu.VMEM((2, STAGE_ROWS, wn), F32),
                        pltpu.SemaphoreType.DMA((2,))],
        compiler_params=_cparams("arbitrary"),
        name=name,
    )(*row_args, *sample_args, *consts, w_stack)
    return res[:n_out], res[n_out:]


def _out_proj_body(w_ref, g, x, mix_a, mix_b):
    half = mix_a.shape[1]
    y = x + _dot(mix_a, w_ref[0:half, :]) + _dot(mix_b, w_ref[half:2 * half, :])
    return y, _rmsnorm(y, g)


def out_proj(x, mix_a, mix_b, xs, mix_a_s, mix_b_s, norm_next, w_o, layer, tm):
    d = x.shape[1]
    return _resident_call(_out_proj_body, [x, mix_a, mix_b], [xs, mix_a_s, mix_b_s], [norm_next.reshape(1, d)],
                          w_o, layer, [(d, F32), (d, BF16)], tm, "out_proj")


def _ffn_out_body(w_ref, g, x, act):
    y = x + _dot(act, w_ref[...])
    return y, _rmsnorm(y, g)


def ffn_out(x, act, xs, act_s, norm_next, w_ffn_out, layer, tm):
    d = x.shape[1]
    return _resident_call(_ffn_out_body, [x, act], [xs, act_s], [norm_next.reshape(1, d)],
                          w_ffn_out, layer, [(d, F32), (d, BF16)], tm, "ffn_out")


def _ple_body(w_ref, wp, g_fin, x, h, p, *, final):
    gate = _sigmoid(_dot(h, w_ref[...]))
    emb = _dot(p.astype(BF16), wp.astype(BF16))
    y = x + gate * emb
    return (_rmsnorm(y, g_fin),) if final else (y,)


def ple(x, h, p, xs, hs, ps, w_ple_gate, w_ple_proj_l, norm_final, layer, final, tm):
    d = x.shape[1]
    return _resident_call(functools.partial(_ple_body, final=final), [x, h, p], [xs, hs, ps],
                          [w_ple_proj_l, norm_final.reshape(1, d)], w_ple_gate, layer, [(d, F32)], tm, "ple")


def _ffn_in_kernel(h_ref, hs_ref, wg_ref, wu_ref, o_ref, os_ref, wgb_ref, wub_ref):
    i = pl.program_id(1)

    @pl.when(i == 0)
    def _():
        wgb_ref[...] = wg_ref[...].astype(BF16)
        wub_ref[...] = wu_ref[...].astype(BF16)

    def swiglu(h):
        a = _dot(h, wgb_ref[...])
        u = _dot(h, wub_ref[...])
        return (a * _sigmoid(a) * u).astype(BF16)

    o_ref[...] = swiglu(h_ref[...])

    @pl.when(i == pl.num_programs(1) - 1)
    def _():
        os_ref[...] = swiglu(hs_ref[...])


def ffn_in(h, hs, w_ffn_in, layer, tm, tn):
    m, k = h.shape
    ms = hs.shape[0]
    d_ff = w_ffn_in.shape[2] // 2
    nj = d_ff // tn
    return pl.pallas_call(
        _ffn_in_kernel,
        grid=(nj, m // tm),
        in_specs=[
            pl.BlockSpec((tm, k), lambda j, i: (i, 0)),
            pl.BlockSpec((ms, k), lambda j, i: (0, 0)),
            pl.BlockSpec((None, k, tn), lambda j, i: (layer, 0, j)),
            pl.BlockSpec((None, k, tn), lambda j, i: (layer, 0, j + nj)),
        ],
        out_specs=[
            pl.BlockSpec((tm, tn), lambda j, i: (i, j)),
            pl.BlockSpec((ms, tn), lambda j, i: (0, j)),
        ],
        out_shape=[jax.ShapeDtypeStruct((m, d_ff), BF16), jax.ShapeDtypeStruct((ms, d_ff), BF16)],
        scratch_shapes=[pltpu.VMEM((k, tn), BF16), pltpu.VMEM((k, tn), BF16)],
        compiler_params=_cparams("arbitrary", "arbitrary"),
        name="ffn_in",
    )(h, hs, w_ffn_in, w_ffn_in)


def _mla_prep_kernel(cq_ref, ckv_ref, tail_ref, cos_ref, sin_ref, qn_ref, kvn_ref, wqb_ref, wkb_ref,
                     q_ref, ckv_out_ref, kpe_out_ref, kvcat_ref):
    scale = (MLA_NOPE + MLA_ROPE) ** -0.5
    cos = cos_ref[...]
    sin = sin_ref[...]
    cqn = _rmsnorm(cq_ref[...], qn_ref[...]).astype(BF16)
    qb = _dot(cqn, wqb_ref[...])
    for h in range(MLA_HEADS):
        base = h * Q_HEAD_COLS
        nope = qb[:, base:base + MLA_NOPE]
        pe = qb[:, base + MLA_NOPE:base + MLA_NOPE + MLA_ROPE]
        pe_rot = qb[:, base + MLA_NOPE + MLA_ROPE:base + Q_HEAD_COLS]
        q_lat = _dot(nope.astype(BF16), wkb_ref[h])
        q_pe = pe * cos + pe_rot * sin
        q_ref[h, :, 0:MLA_KV_RANK] = (q_lat * scale).astype(BF16)
        q_ref[h, :, MLA_KV_RANK:QK_WIDTH] = (q_pe * scale).astype(BF16)
        q_ref[h, :, QK_WIDTH:QK_PAD] = jnp.zeros((q_pe.shape[0], QK_PAD - QK_WIDTH), BF16)
    ckv = _rmsnorm(ckv_ref[...], kvn_ref[...])
    tail = tail_ref[...]
    kpe = (tail[:, TAIL_KPE:TAIL_KPE + MLA_ROPE] * cos
           + tail[:, TAIL_KPE_ROT:TAIL_KPE_ROT + MLA_ROPE] * sin)
    ckv_out_ref[...] = ckv
    kpe_out_ref[...] = kpe
    kvcat_ref[:, 0:MLA_KV_RANK] = ckv.astype(BF16)
    kvcat_ref[:, MLA_KV_RANK:QK_WIDTH] = kpe.astype(BF16)
    kvcat_ref[:, QK_WIDTH:QK_PAD] = jnp.zeros((kpe.shape[0], QK_PAD - QK_WIDTH), BF16)


def mla_prep(z, cos, sin, q_a_norm, kv_a_norm, wqb, wkb_t, tm):
    m = z.shape[0]
    n_pos_blocks = cos.shape[0] // tm
    return pl.pallas_call(
        _mla_prep_kernel,
        grid=(m // tm,),
        in_specs=[
            pl.BlockSpec((tm, MLA_Q_RANK), lambda i: (i, Z_CQ // MLA_Q_RANK)),
            pl.BlockSpec((tm, MLA_KV_RANK), lambda i: (i, Z_CKV // MLA_KV_RANK)),
            pl.BlockSpec((tm, TAIL_W), lambda i: (i, Z_TAIL // TAIL_W)),
            pl.BlockSpec((tm, MLA_ROPE), lambda i: (i % n_pos_blocks, 0)),
            pl.BlockSpec((tm, MLA_ROPE), lambda i: (i % n_pos_blocks, 0)),
            pl.BlockSpec((1, MLA_Q_RANK), lambda i: (0, 0)),
            pl.BlockSpec((1, MLA_KV_RANK), lambda i: (0, 0)),
            pl.BlockSpec(wqb.shape, lambda i: (0, 0)),
            pl.BlockSpec(wkb_t.shape, lambda i: (0, 0, 0)),
        ],
        out_specs=[
            pl.BlockSpec((MLA_HEADS, tm, QK_PAD), lambda i: (0, i, 0)),
            pl.BlockSpec((tm, MLA_KV_RANK), lambda i: (i, 0)),
            pl.BlockSpec((tm, MLA_ROPE), lambda i: (i, 0)),
            pl.BlockSpec((tm, QK_PAD), lambda i: (i, 0)),
        ],
        out_shape=[
            jax.ShapeDtypeStruct((MLA_HEADS, m, QK_PAD), BF16),
            jax.ShapeDtypeStruct((m, MLA_KV_RANK), F32),
            jax.ShapeDtypeStruct((m, MLA_ROPE), F32),
            jax.ShapeDtypeStruct((m, QK_PAD), BF16),
        ],
        compiler_params=_cparams("parallel"),
        name="mla_prep",
    )(z, z, z, cos, sin, q_a_norm.reshape(1, -1), kv_a_norm.reshape(1, -1), wqb, wkb_t)


def _mla_prompt_kernel(q_ref, kv_ref, wvb_ref, o_ref, olat_ref, *, tq, nq):
    qi = pl.program_id(1)
    t = lax.broadcasted_iota(jnp.int32, (tq, tq), 0)
    c = lax.broadcasted_iota(jnp.int32, (tq, tq), 1)
    causal = c <= t

    for v in range(nq):
        n_past = v * tq

        @pl.when(qi == v)
        def _(n_past=n_past):
            def head(h, carry):
                q = q_ref[h]
                kv_d = kv_ref[n_past:n_past + tq, :]
                s_d = jnp.where(causal, _dot_nt(q, kv_d), -jnp.inf)
                m = jnp.max(s_d, axis=-1, keepdims=True)
                if n_past:
                    s_p = _dot_nt(q, kv_ref[0:n_past, :])
                    m = jnp.maximum(m, jnp.max(s_p, axis=-1, keepdims=True))
                p_d = jnp.exp(s_d - m)
                l = jnp.sum(p_d, axis=-1, keepdims=True)
                o = _dot(p_d.astype(BF16), kv_d[:, 0:MLA_KV_RANK])
                if n_past:
                    p_p = jnp.exp(s_p - m)
                    l = l + jnp.sum(p_p, axis=-1, keepdims=True)
                    o = o + _dot(p_p.astype(BF16), kv_ref[0:n_past, 0:MLA_KV_RANK])
                olat_ref[h] = o / l
                return carry

            lax.fori_loop(0, MLA_HEADS, head, 0, unroll=4)

    for h in range(MLA_HEADS):
        o_ref[:, h * MLA_DV:(h + 1) * MLA_DV] = _dot(olat_ref[h].astype(BF16), wvb_ref[h]).astype(BF16)


def mla_prompt_attention(q, kvcat, wvb, batch, seq, tq):
    nq = seq // tq
    return pl.pallas_call(
        functools.partial(_mla_prompt_kernel, tq=tq, nq=nq),
        grid=(batch, nq),
        in_specs=[
            pl.BlockSpec((MLA_HEADS, tq, QK_PAD), lambda b, i: (0, b * nq + i, 0)),
            pl.BlockSpec((seq, QK_PAD), lambda b, i: (b, 0)),
            pl.BlockSpec(wvb.shape, lambda b, i: (0, 0, 0)),
        ],
        out_specs=pl.BlockSpec((tq, MLA_HEADS * MLA_DV), lambda b, i: (b * nq + i, 0)),
        out_shape=jax.ShapeDtypeStruct((batch * seq, MLA_HEADS * MLA_DV), BF16),
        scratch_shapes=[pltpu.VMEM((MLA_HEADS, tq, MLA_KV_RANK), F32)],
        compiler_params=_cparams("parallel", "arbitrary"),
        name="mla_prompt_attn",
    )(q, kvcat, wvb)


ROWS_PER_STEP = 2


def _mla_decode_kernel(pt_ref, q_ref, kvn_ref, wvb_ref, ckv_hbm, kpet_hbm, o_ref,
                       ckv_buf, kpet_buf, sem, olat_ref, *, layer, page, n_pages):
    i = pl.program_id(0)
    n = pl.num_programs(0)

    def copies(row, slot):
        out = []
        for j in range(n_pages):
            pg = pt_ref[row, j]
            keys = pl.ds(j * page, page)
            out.append(pltpu.make_async_copy(ckv_hbm.at[layer, pg], ckv_buf.at[slot, keys], sem.at[slot]))
            out.append(pltpu.make_async_copy(kpet_hbm.at[layer, pg], kpet_buf.at[slot, :, keys], sem.at[slot]))
        return out

    def attend(r, slot):
        q = q_ref[r]
        kvn = kvn_ref[r]
        ckv = ckv_buf[slot].astype(BF16)
        kpet = kpet_buf[slot].astype(BF16)
        s = _dot_nt(q[:, 0:MLA_KV_RANK], ckv) + _dot(q[:, MLA_KV_RANK:QK_WIDTH], kpet)
        s_new = jnp.sum(q.astype(F32) * kvn.astype(F32), axis=-1, keepdims=True)
        m = jnp.maximum(jnp.max(s, axis=-1, keepdims=True), s_new)
        p = jnp.exp(s - m)
        p_new = jnp.exp(s_new - m)
        l = jnp.sum(p, axis=-1, keepdims=True) + p_new
        acc = _dot(p.astype(BF16), ckv) + p_new * kvn[:, 0:MLA_KV_RANK].astype(F32)
        o_lat = acc / l
        row = i * ROWS_PER_STEP + r
        for h in range(MLA_HEADS):
            olat_ref[h, pl.ds(row, 1), :] = o_lat[h:h + 1]

    @pl.when(i == 0)
    def _():
        for cp in copies(0, 0):
            cp.start()

    for r in range(ROWS_PER_STEP):
        row = i * ROWS_PER_STEP + r
        if r + 1 < ROWS_PER_STEP:
            for cp in copies(row + 1, r + 1):
                cp.start()
        else:
            @pl.when(i + 1 < n)
            def _():
                for cp in copies(row + 1, 0):
                    cp.start()
        for cp in copies(row, r):
            cp.wait()
        attend(r, r)

    @pl.when(i == n - 1)
    def _():
        for h in range(MLA_HEADS):
            o_ref[:, h * MLA_DV:(h + 1) * MLA_DV] = _dot(olat_ref[h].astype(BF16), wvb_ref[h]).astype(BF16)


def mla_decode_attention(page_table, q, kvcat_new, wvb, cache_ckv, cache_kpe_t, layer):
    nb, n_pages = page_table.shape
    page = cache_ckv.shape[2]
    assert nb % ROWS_PER_STEP == 0
    keys = n_pages * page
    grid_spec = pltpu.PrefetchScalarGridSpec(
        num_scalar_prefetch=1,
        grid=(nb // ROWS_PER_STEP,),
        in_specs=[
            pl.BlockSpec((ROWS_PER_STEP, MLA_HEADS, QK_PAD), lambda i, pt: (i, 0, 0)),
            pl.BlockSpec((ROWS_PER_STEP, 1, QK_PAD), lambda i, pt: (i, 0, 0)),
            pl.BlockSpec(wvb.shape, lambda i, pt: (0, 0, 0)),
            pl.BlockSpec(memory_space=pl.ANY),
            pl.BlockSpec(memory_space=pl.ANY),
        ],
        out_specs=pl.BlockSpec((nb, MLA_HEADS * MLA_DV), lambda i, pt: (0, 0)),
        scratch_shapes=[
            pltpu.VMEM((ROWS_PER_STEP, keys, MLA_KV_RANK), F32),
            pltpu.VMEM((ROWS_PER_STEP, MLA_ROPE, keys), F32),
            pltpu.SemaphoreType.DMA((ROWS_PER_STEP,)),
            pltpu.VMEM((MLA_HEADS, nb, MLA_KV_RANK), F32),
        ],
    )
    return pl.pallas_call(
        functools.partial(_mla_decode_kernel, layer=layer, page=page, n_pages=n_pages),
        grid_spec=grid_spec,
        out_shape=jax.ShapeDtypeStruct((nb, MLA_HEADS * MLA_DV), BF16),
        compiler_params=_cparams("arbitrary"),
        name="mla_decode_attn",
    )(page_table, q, kvcat_new.reshape(nb, 1, QK_PAD), wvb, cache_ckv, cache_kpe_t)


def _gla_prompt_kernel(q_ref, k_ref, v_ref, g_ref, tail_ref, wgk_ref, bgk_ref, gn_ref,
                       o_ref, sfin_ref, st_ref, *, ct):
    ci = pl.program_id(2)

    @pl.when(ci == 0)
    def _():
        st_ref[...] = jnp.zeros_like(st_ref)

    pre = _dot(tail_ref[...].astype(BF16), wgk_ref[...]) + bgk_ref[...]
    gk_all = _log_sigmoid(pre) * (1.0 / GLA_GATE_NORM)
    row = lax.broadcasted_iota(jnp.int32, (GLA_CHUNK, GLA_CHUNK), 0)
    col = lax.broadcasted_iota(jnp.int32, (GLA_CHUNK, GLA_CHUNK), 1)
    causal = col <= row
    tri = causal.astype(F32)
    gn = gn_ref[...]

    for c in range(ct // GLA_CHUNK):
        rows = slice(c * GLA_CHUNK, (c + 1) * GLA_CHUNK)
        for hh in range(GLA_HEADS_PER_STEP):
            ks = slice(hh * GLA_DK, (hh + 1) * GLA_DK)
            vs = slice(hh * GLA_DV, (hh + 1) * GLA_DV)
            b = jnp.dot(tri, gk_all[rows, ks], preferred_element_type=F32, precision=lax.Precision.HIGHEST)
            b_last = b[GLA_CHUNK - 1:GLA_CHUNK]
            q = q_ref[rows, ks] * (GLA_DK ** -0.5)
            k = k_ref[rows, ks]
            v = v_ref[rows, vs].astype(BF16)
            st = st_ref[hh]
            o = _dot_nt((q * jnp.exp(b)).astype(BF16), st.astype(BF16))
            parts = []
            for i in range(GLA_CHUNK // GLA_SUB):
                sub = slice(i * GLA_SUB, (i + 1) * GLA_SUB)
                r = b[i * GLA_SUB:i * GLA_SUB + 1]
                q_i = (q[sub] * jnp.exp(b[sub] - r)).astype(BF16)
                k_i = (k * jnp.exp(r - b)).astype(BF16)
                parts.append(_dot_nt(q_i, k_i))
            attn = jnp.where(causal, jnp.concatenate(parts, axis=0), 0.0)
            o = o + _dot(attn.astype(BF16), v)
            k_hat = (k * jnp.exp(b_last - b)).astype(BF16)
            st_ref[hh] = st * jnp.exp(b_last) + _dot_tn(v, k_hat)
            o = _rmsnorm(o, gn)
            g = g_ref[rows, vs]
            o_ref[rows, vs] = (o * (g * _sigmoid(g))).astype(BF16)

    @pl.when(ci == pl.num_programs(2) - 1)
    def _():
        for hh in range(GLA_HEADS_PER_STEP):
            sfin_ref[0, hh] = st_ref[hh].T


def gla_prompt(z, wgk, bgk, gla_norm, batch, seq, ct):
    m = z.shape[0]
    nc = seq // ct
    hp = GLA_HEADS_PER_STEP
    kw, vw = hp * GLA_DK, hp * GLA_DV
    row = lambda b, h, c: b * nc + c
    return pl.pallas_call(
        functools.partial(_gla_prompt_kernel, ct=ct),
        grid=(batch, GLA_HEADS // hp, nc),
        in_specs=[
            pl.BlockSpec((ct, kw), lambda b, h, c: (row(b, h, c), Z_Q // kw + h)),
            pl.BlockSpec((ct, kw), lambda b, h, c: (row(b, h, c), Z_K // kw + h)),
            pl.BlockSpec((ct, vw), lambda b, h, c: (row(b, h, c), Z_V // vw + h)),
            pl.BlockSpec((ct, vw), lambda b, h, c: (row(b, h, c), Z_G // vw + h)),
            pl.BlockSpec((ct, TAIL_W), lambda b, h, c: (row(b, h, c), Z_TAIL // TAIL_W)),
            pl.BlockSpec((TAIL_W, kw), lambda b, h, c: (0, h)),
            pl.BlockSpec((1, kw), lambda b, h, c: (0, h)),
            pl.BlockSpec((1, GLA_DV), lambda b, h, c: (0, 0)),
        ],
        out_specs=[
            pl.BlockSpec((ct, vw), lambda b, h, c: (row(b, h, c), h)),
            pl.BlockSpec((1, hp, GLA_DK, GLA_DV), lambda b, h, c: (b, h, 0, 0)),
        ],
        out_shape=[
            jax.ShapeDtypeStruct((m, GLA_HEADS * GLA_DV), BF16),
            jax.ShapeDtypeStruct((batch, GLA_HEADS, GLA_DK, GLA_DV), F32),
        ],
        scratch_shapes=[pltpu.VMEM((hp, GLA_DV, GLA_DK), F32)],
        compiler_params=_cparams("parallel", "parallel", "arbitrary"),
        name="gla_prompt",
    )(z, z, z, z, z, wgk, bgk.reshape(1, -1), gla_norm.reshape(1, -1))


def _gla_decode_kernel(q_ref, k_ref, v_ref, g_ref, tail_ref, wgk_ref, bgk_ref, gn_ref, s_ref,
                       o_ref, snew_ref, *, tb):
    pre = _dot(tail_ref[...].astype(BF16), wgk_ref[...]) + bgk_ref[...]
    decay = jnp.exp(_log_sigmoid(pre) * (1.0 / GLA_GATE_NORM))
    q_all = q_ref[...] * (GLA_DK ** -0.5)
    k_all = k_ref[...]
    v_all = v_ref[...]
    g_all = g_ref[...]
    gn = gn_ref[...]
    eye = (lax.broadcasted_iota(jnp.int32, (GLA_DK, GLA_DK), 0)
           == lax.broadcasted_iota(jnp.int32, (GLA_DK, GLA_DK), 1))

    def column(r):
        return jnp.sum(jnp.where(eye, r, 0.0), axis=1, keepdims=True)

    for t in range(tb):
        for h in range(GLA_HEADS):
            ks = slice(h * GLA_DK, (h + 1) * GLA_DK)
            vs = slice(h * GLA_DV, (h + 1) * GLA_DV)
            q_c = column(q_all[t:t + 1, ks])
            k_c = column(k_all[t:t + 1, ks])
            d_c = column(decay[t:t + 1, ks])
            s_new = d_c * s_ref[t, h] + k_c * v_all[t:t + 1, vs]
            snew_ref[t, h] = s_new
            o = jnp.sum(q_c * s_new, axis=0, keepdims=True)
            o = _rmsnorm(o, gn)
            g = g_all[t:t + 1, vs]
            o_ref[t:t + 1, vs] = (o * (g * _sigmoid(g))).astype(BF16)


def gla_decode(z, wgk, bgk, gla_norm, state_all, layer, tb):
    m = z.shape[0]
    qk_w = GLA_HEADS * GLA_DK
    v_w = GLA_HEADS * GLA_DV
    return pl.pallas_call(
        functools.partial(_gla_decode_kernel, tb=tb),
        grid=(m // tb,),
        in_specs=[
            pl.BlockSpec((tb, qk_w), lambda i: (i, Z_Q // qk_w)),
            pl.BlockSpec((tb, qk_w), lambda i: (i, Z_K // qk_w)),
            pl.BlockSpec((tb, v_w), lambda i: (i, Z_V // v_w)),
            pl.BlockSpec((tb, v_w), lambda i: (i, Z_G // v_w)),
            pl.BlockSpec((tb, TAIL_W), lambda i: (i, Z_TAIL // TAIL_W)),
            pl.BlockSpec((TAIL_W, qk_w), lambda i: (0, 0)),
            pl.BlockSpec((1, qk_w), lambda i: (0, 0)),
            pl.BlockSpec((1, GLA_DV), lambda i: (0, 0)),
            pl.BlockSpec((None, tb, GLA_HEADS, GLA_DK, GLA_DV), lambda i: (layer, i, 0, 0, 0)),
        ],
        out_specs=[
            pl.BlockSpec((tb, v_w), lambda i: (i, 0)),
            pl.BlockSpec((tb, GLA_HEADS, GLA_DK, GLA_DV), lambda i: (i, 0, 0, 0)),
        ],
        out_shape=[
            jax.ShapeDtypeStruct((m, v_w), BF16),
            jax.ShapeDtypeStruct(state_all.shape[1:], F32),
        ],
        compiler_params=_cparams("parallel"),
        name="gla_decode",
    )(z, z, z, z, z, wgk, bgk.reshape(1, -1), gla_norm.reshape(1, -1), state_all)


def _rot_cols(w):
    half = w.shape[-1] // 2
    return jnp.concatenate([-w[..., half:], w[..., :half]], axis=-1)


def _prep_layer_weights(w_in, w_gk2, w_qb, w_kb, w_vb):
    d = w_in.shape[0]
    qk = GLA_HEADS * GLA_DK
    vw = GLA_HEADS * GLA_DV
    o_q, o_k, o_v = 0, qk, 2 * qk
    o_glr = o_v + vw
    o_g = o_glr + GLA_GATE_RANK
    o_cq = o_g + vw
    o_ckv = o_cq + MLA_Q_RANK
    o_kpe = o_ckv + MLA_KV_RANK
    w_kpe = w_in[:, o_kpe:o_kpe + MLA_ROPE]
    pad = jnp.zeros((d, TAIL_W - 2 * MLA_ROPE - GLA_GATE_RANK), F32)
    w_in_r = jnp.concatenate([
        w_in[:, o_q:o_q + qk], w_in[:, o_k:o_k + qk], w_in[:, o_v:o_v + vw], w_in[:, o_g:o_g + vw],
        w_in[:, o_cq:o_cq + MLA_Q_RANK], w_in[:, o_ckv:o_ckv + MLA_KV_RANK],
        w_kpe, _rot_cols(w_kpe), w_in[:, o_glr:o_glr + GLA_GATE_RANK], pad], axis=1).astype(BF16)
    wgk = jnp.zeros((TAIL_W, qk), F32).at[TAIL_GLR:TAIL_GLR + GLA_GATE_RANK].set(w_gk2).astype(BF16)
    wq = w_qb.reshape(MLA_Q_RANK, MLA_HEADS, MLA_NOPE + MLA_ROPE)
    wq_pe = wq[..., MLA_NOPE:]
    wqb_r = jnp.concatenate([wq[..., :MLA_NOPE], wq_pe, _rot_cols(wq_pe)], axis=-1)
    wqb_r = wqb_r.reshape(MLA_Q_RANK, MLA_HEADS * Q_HEAD_COLS).astype(BF16)
    wkb_t = jnp.transpose(w_kb, (1, 2, 0)).astype(BF16)
    wvb_h = jnp.transpose(w_vb, (1, 0, 2)).astype(BF16)
    return dict(w_in=w_in_r, wgk=wgk, wqb=wqb_r, wkb=wkb_t, wvb=wvb_h)


def _rope_tables(pos):
    freqs = ROPE_THETA ** (-jnp.arange(0, MLA_ROPE, 2, dtype=F32) / MLA_ROPE)
    ang = pos[:, None] * freqs[None, :]
    cos, sin = jnp.cos(ang), jnp.sin(ang)
    return jnp.concatenate([cos, cos], axis=-1), jnp.concatenate([sin, sin], axis=-1)


def _pick(m, target):
    return min(m, target)


def kernel(x_prompt, x_sample, state_gla, cache_ckv, cache_kpe, page_table, p_prompt, p_sample,
           norm_mix, w_in, w_gk2, b_gk, gla_norm, q_a_norm, kv_a_norm, w_qb, w_kb, w_vb, w_o,
           norm_ffn, w_ffn_in, w_ffn_out, norm_ple, w_ple_gate, w_ple_proj, norm_final):
    bsz, seq, d = x_prompt.shape
    nb, ls, _ = x_sample.shape
    assert ls == 1
    depth = w_in.shape[0]
    past = page_table.shape[1] * cache_ckv.shape[2]
    mp = bsz * seq
    tm_p = _pick(mp, 1024)

    cache_kpe_t = jnp.swapaxes(cache_kpe, 2, 3)
    cos_p, sin_p = _rope_tables(jnp.arange(seq, dtype=F32))
    cos_s, sin_s = _rope_tables(jnp.full((nb,), float(past), F32))

    xp = x_prompt.reshape(mp, d)
    xs = x_sample.reshape(nb, d)
    gla_p, gla_s, ckv_p, kpe_p, ckv_s, kpe_s = [], [], [], [], [], []
    for i in range(depth):
        w = _prep_layer_weights(w_in[i], w_gk2[i], w_qb[i], w_kb[i], w_vb[i])
        z = norm_matmul(xp, norm_mix[i], w["w_in"], tm_p, 1024)
        mix_a, s_fin = gla_prompt(z, w["wgk"], b_gk[i], gla_norm[i], bsz, seq, _pick(seq, 512))
        q, ckv, kpe, kvcat = mla_prep(z, cos_p, sin_p, q_a_norm[i], kv_a_norm[i], w["wqb"], w["wkb"],
                                      _pick(seq, 512))
        mix_b = mla_prompt_attention(q, kvcat, w["wvb"], bsz, seq, _pick(seq, 256))
        gla_p.append(s_fin)
        ckv_p.append(ckv.reshape(bsz, seq, MLA_KV_RANK))
        kpe_p.append(kpe.reshape(bsz, seq, MLA_ROPE))
        zs = norm_matmul(xs, norm_mix[i], w["w_in"], nb, 1024)
        mix_a_s, s_new = gla_decode(zs, w["wgk"], b_gk[i], gla_norm[i], state_gla, i, 8)
        gla_s.append(s_new)
        q_s, ckv_n, kpe_n, kvcat_n = mla_prep(zs, cos_s, sin_s, q_a_norm[i], kv_a_norm[i], w["wqb"],
                                              w["wkb"], nb)
        q_s = jnp.transpose(q_s, (1, 0, 2))
        mix_b_s = mla_decode_attention(page_table, q_s, kvcat_n, w["wvb"], cache_ckv, cache_kpe_t, i)
        ckv_s.append(ckv_n.reshape(nb, ls, MLA_KV_RANK))
        kpe_s.append(kpe_n.reshape(nb, ls, MLA_ROPE))
        (xp, hp), (xs, hs) = out_proj(xp, mix_a, mix_b, xs, mix_a_s, mix_b_s, norm_ffn[i], w_o, i, _pick(mp, 512))
        act, act_s = ffn_in(hp, hs, w_ffn_in, i, tm_p, 512)
        (xp, hp), (xs, hs) = ffn_out(xp, act, xs, act_s, norm_ple[i], w_ffn_out, i, _pick(mp, 256))
        (xp,), (xs,) = ple(xp, hp, p_prompt[i].reshape(mp, -1), xs, hs, p_sample[i].reshape(nb, -1),
                           w_ple_gate, w_ple_proj[i], norm_final, i, i == depth - 1, _pick(mp, 512))

    y_prompt = xp.reshape(bsz, seq, d)
    y_sample = xs.reshape(nb, ls, d)
    return (y_prompt, y_sample, jnp.stack(gla_p), jnp.stack(gla_s), jnp.stack(ckv_p),
            jnp.stack(kpe_p), jnp.stack(ckv_s), jnp.stack(kpe_s))
```

```python
import functools

import jax
import jax.numpy as jnp
import numpy as np
from jax import lax
from jax.experimental import pallas as pl
from jax.experimental.pallas import tpu as pltpu

F32 = jnp.float32
BF16 = jnp.bfloat16

GLA_HEADS = 4
GLA_DK = 128
GLA_DV = 256
GLA_GATE_RANK = 16
GLA_GATE_NORM = 16.0
MLA_HEADS = 8
MLA_Q_RANK = 512
MLA_KV_RANK = 256
MLA_NOPE = 128
MLA_ROPE = 64
MLA_DV = 128
ROPE_THETA = 10000.0
EPS = 1e-6

QK_WIDTH = MLA_KV_RANK + MLA_ROPE
QK_PAD = 384
Q_HEAD_COLS = MLA_NOPE + 2 * MLA_ROPE

Z_Q = 0
Z_K = Z_Q + GLA_HEADS * GLA_DK
Z_V = Z_K + GLA_HEADS * GLA_DK
Z_G = Z_V + GLA_HEADS * GLA_DV
Z_CQ = Z_G + GLA_HEADS * GLA_DV
Z_CKV = Z_CQ + MLA_Q_RANK
Z_TAIL = Z_CKV + MLA_KV_RANK
TAIL_W = 256
TAIL_KPE = 0
TAIL_KPE_ROT = MLA_ROPE
TAIL_GLR = 2 * MLA_ROPE
Z_WIDTH = Z_TAIL + TAIL_W

VMEM_LIMIT_BYTES = 56 * 1024 * 1024

GLA_CHUNK = 128
GLA_SUB = 16
GLA_HEADS_PER_STEP = 2


def _cparams(*sem):
    return pltpu.CompilerParams(dimension_semantics=sem, vmem_limit_bytes=VMEM_LIMIT_BYTES)


def _rmsnorm(x, g):
    ms = jnp.mean(x * x, axis=-1, keepdims=True)
    return x * lax.rsqrt(ms + EPS) * g


def _sigmoid(x):
    return 1.0 / (1.0 + jnp.exp(-x))


def _log_sigmoid(x):
    return jnp.minimum(x, 0.0) - jnp.log1p(jnp.exp(-jnp.abs(x)))


def _dot(a, b):
    return jnp.dot(a, b, preferred_element_type=F32)


def _dot_nt(a, b):
    return lax.dot_general(a, b, (((1,), (1,)), ((), ())), preferred_element_type=F32)


def _dot_tn(a, b):
    return lax.dot_general(a, b, (((0,), (0,)), ((), ())), preferred_element_type=F32)


def _norm_mm_kernel(x_ref, g_ref, w_ref, o_ref, h_ref):
    @pl.when(pl.program_id(1) == 0)
    def _():
        h_ref[...] = _rmsnorm(x_ref[...], g_ref[...]).astype(BF16)

    o_ref[...] = _dot_nt(h_ref[...], w_ref[...])


def norm_matmul(x, g, w_t, tm, tn):
    m, k = x.shape
    n = w_t.shape[0]
    return pl.pallas_call(
        _norm_mm_kernel,
        grid=(m // tm, n // tn),
        in_specs=[
            pl.BlockSpec((tm, k), lambda i, j: (i, 0)),
            pl.BlockSpec((1, k), lambda i, j: (0, 0)),
            pl.BlockSpec((tn, k), lambda i, j: (j, 0)),
        ],
        out_specs=pl.BlockSpec((tm, tn), lambda i, j: (i, j)),
        out_shape=jax.ShapeDtypeStruct((m, n), F32),
        scratch_shapes=[pltpu.VMEM((tm, k), BF16)],
        compiler_params=_cparams("parallel", "arbitrary"),
        name="in_proj",
    )(x, g.reshape(1, k), w_t)


STAGE_ROWS = 256


def _stage_cast(w_hbm, stage_ref, sem, dst_ref):
    n = dst_ref.shape[0] // STAGE_ROWS

    def copy(c):
        return pltpu.make_async_copy(w_hbm.at[pl.ds(c * STAGE_ROWS, STAGE_ROWS)], stage_ref.at[c % 2], sem.at[c % 2])

    copy(0).start()
    for c in range(n):
        if c + 1 < n:
            copy(c + 1).start()
        copy(c).wait()
        dst_ref[c * STAGE_ROWS:(c + 1) * STAGE_ROWS, :] = stage_ref[c % 2].astype(BF16)


def _resident_call(body, row_args, sample_args, consts, w_stack, layer, outs, tm, name):
    m = row_args[0].shape[0]
    ms = sample_args[0].shape[0]
    n_row, n_const, n_out = len(row_args), len(consts), len(outs)
    wk, wn = w_stack.shape[1:]

    def kern(*refs):
        rows = refs[0:n_row]
        srows = refs[n_row:2 * n_row]
        cs = refs[2 * n_row:2 * n_row + n_const]
        w_hbm = refs[2 * n_row + n_const]
        o = refs[2 * n_row + n_const + 1:][:n_out]
        so = refs[2 * n_row + n_const + 1 + n_out:][:n_out]
        w_ref, stage_ref, sem = refs[-3:]
        i = pl.program_id(0)

        @pl.when(i == 0)
        def _():
            _stage_cast(w_hbm.at[layer], stage_ref, sem, w_ref)

        for r, val in zip(o, body(w_ref, *[c[...] for c in cs], *[r[...] for r in rows])):
            r[...] = val.astype(r.dtype)

        @pl.when(i == pl.num_programs(0) - 1)
        def _():
            for r, val in zip(so, body(w_ref, *[c[...] for c in cs], *[r[...] for r in srows])):
                r[...] = val.astype(r.dtype)

    in_specs = ([pl.BlockSpec((tm, a.shape[1]), lambda i: (i, 0)) for a in row_args]
                + [pl.BlockSpec(a.shape, lambda i: (0, 0)) for a in sample_args]
                + [pl.BlockSpec(c.shape, lambda i, nd=c.ndim: (0,) * nd) for c in consts]
                + [pl.BlockSpec(memory_space=pl.ANY)])
    out_specs = ([pl.BlockSpec((tm, w), lambda i: (i, 0)) for w, _ in outs]
                 + [pl.BlockSpec((ms, w), lambda i: (0, 0)) for w, _ in outs])
    out_shape = ([jax.ShapeDtypeStruct((m, w), dt) for w, dt in outs]
                 + [jax.ShapeDtypeStruct((ms, w), dt) for w, dt in outs])
    res = pl.pallas_call(
        kern,
        grid=(m // tm,),
        in_specs=in_specs,
        out_specs=out_specs,
        out_shape=out_shape,
        scratch_shapes=[pltpu.VMEM((wk, wn), BF16), pltpu.VMEM((2, STAGE_ROWS, wn), F32),
                        pltpu.SemaphoreType.DMA((2,))],
        compiler_params=_cparams("arbitrary"),
        name=name,
    )(*row_args, *sample_args, *consts, w_stack)
    return res[:n_out], res[n_out:]


def _out_proj_body(w_ref, g, x, mix_a, mix_b):
    half = mix_a.shape[1]
    y = x + _dot(mix_a, w_ref[0:half, :]) + _dot(mix_b, w_ref[half:2 * half, :])
    return y, _rmsnorm(y, g)


def out_proj(x, mix_a, mix_b, xs, mix_a_s, mix_b_s, norm_next, w_o, layer, tm):
    d = x.shape[1]
    return _resident_call(_out_proj_body, [x, mix_a, mix_b], [xs, mix_a_s, mix_b_s], [norm_next.reshape(1, d)],
                          w_o, layer, [(d, F32), (d, BF16)], tm, "out_proj")


def _ffn_out_body(w_ref, g, x, act):
    y = x + _dot(act, w_ref[...])
    return y, _rmsnorm(y, g)


def ffn_out(x, act, xs, act_s, norm_next, w_ffn_out, layer, tm):
    d = x.shape[1]
    return _resident_call(_ffn_out_body, [x, act], [xs, act_s], [norm_next.reshape(1, d)],
                          w_ffn_out, layer, [(d, F32), (d, BF16)], tm, "ffn_out")


def _ple_body(w_ref, wp, g_fin, x, h, p, *, final):
    gate = _sigmoid(_dot(h, w_ref[...]))
    emb = _dot(p.astype(BF16), wp.astype(BF16))
    y = x + gate * emb
    return (_rmsnorm(y, g_fin),) if final else (y,)


def ple(x, h, p, xs, hs, ps, w_ple_gate, w_ple_proj_l, norm_final, layer, final, tm):
    d = x.shape[1]
    return _resident_call(functools.partial(_ple_body, final=final), [x, h, p], [xs, hs, ps],
                          [w_ple_proj_l, norm_final.reshape(1, d)], w_ple_gate, layer, [(d, F32)], tm, "ple")


def _ffn_in_kernel(h_ref, hs_ref, wg_ref, wu_ref, o_ref, os_ref, wgb_ref, wub_ref):
    i = pl.program_id(1)

    @pl.when(i == 0)
    def _():
        wgb_ref[...] = wg_ref[...].astype(BF16)
        wub_ref[...] = wu_ref[...].astype(BF16)

    def swiglu(h):
        a = _dot(h, wgb_ref[...])
        u = _dot(h, wub_ref[...])
        return (a * _sigmoid(a) * u).astype(BF16)

    o_ref[...] = swiglu(h_ref[...])

    @pl.when(i == pl.num_programs(1) - 1)
    def _():
        os_ref[...] = swiglu(hs_ref[...])


def ffn_in(h, hs, w_ffn_in, layer, tm, tn):
    m, k = h.shape
    ms = hs.shape[0]
    d_ff = w_ffn_in.shape[2] // 2
    nj = d_ff // tn
    return pl.pallas_call(
        _ffn_in_kernel,
        grid=(nj, m // tm),
        in_specs=[
            pl.BlockSpec((tm, k), lambda j, i: (i, 0)),
            pl.BlockSpec((ms, k), lambda j, i: (0, 0)),
            pl.BlockSpec((None, k, tn), lambda j, i: (layer, 0, j)),
            pl.BlockSpec((None, k, tn), lambda j, i: (layer, 0, j + nj)),
        ],
        out_specs=[
            pl.BlockSpec((tm, tn), lambda j, i: (i, j)),
            pl.BlockSpec((ms, tn), lambda j, i: (0, j)),
        ],
        out_shape=[jax.ShapeDtypeStruct((m, d_ff), BF16), jax.ShapeDtypeStruct((ms, d_ff), BF16)],
        scratch_shapes=[pltpu.VMEM((k, tn), BF16), pltpu.VMEM((k, tn), BF16)],
        compiler_params=_cparams("arbitrary", "arbitrary"),
        name="ffn_in",
    )(h, hs, w_ffn_in, w_ffn_in)


def _mla_prep_kernel(cq_ref, ckv_ref, tail_ref, cos_ref, sin_ref, qn_ref, kvn_ref, wqb_ref, wkb_ref,
                     q_ref, ckv_out_ref, kpe_out_ref, kvcat_ref):
    scale = (MLA_NOPE + MLA_ROPE) ** -0.5
    cos = cos_ref[...]
    sin = sin_ref[...]
    cqn = _rmsnorm(cq_ref[...], qn_ref[...]).astype(BF16)
    qb = _dot(cqn, wqb_ref[...])
    for h in range(MLA_HEADS):
        base = h * Q_HEAD_COLS
        nope = qb[:, base:base + MLA_NOPE]
        pe = qb[:, base + MLA_NOPE:base + MLA_NOPE + MLA_ROPE]
        pe_rot = qb[:, base + MLA_NOPE + MLA_ROPE:base + Q_HEAD_COLS]
        q_lat = _dot(nope.astype(BF16), wkb_ref[h])
        q_pe = pe * cos + pe_rot * sin
        q_ref[h, :, 0:MLA_KV_RANK] = (q_lat * scale).astype(BF16)
        q_ref[h, :, MLA_KV_RANK:QK_WIDTH] = (q_pe * scale).astype(BF16)
        q_ref[h, :, QK_WIDTH:QK_PAD] = jnp.zeros((q_pe.shape[0], QK_PAD - QK_WIDTH), BF16)
    ckv = _rmsnorm(ckv_ref[...], kvn_ref[...])
    tail = tail_ref[...]
    kpe = (tail[:, TAIL_KPE:TAIL_KPE + MLA_ROPE] * cos
           + tail[:, TAIL_KPE_ROT:TAIL_KPE_ROT + MLA_ROPE] * sin)
    ckv_out_ref[...] = ckv
    kpe_out_ref[...] = kpe
    kvcat_ref[:, 0:MLA_KV_RANK] = ckv.astype(BF16)
    kvcat_ref[:, MLA_KV_RANK:QK_WIDTH] = kpe.astype(BF16)
    kvcat_ref[:, QK_WIDTH:QK_PAD] = jnp.zeros((kpe.shape[0], QK_PAD - QK_WIDTH), BF16)


def mla_prep(z, cos, sin, q_a_norm, kv_a_norm, wqb, wkb_t, tm):
    m = z.shape[0]
    n_pos_blocks = cos.shape[0] // tm
    return pl.pallas_call(
        _mla_prep_kernel,
        grid=(m // tm,),
        in_specs=[
            pl.BlockSpec((tm, MLA_Q_RANK), lambda i: (i, Z_CQ // MLA_Q_RANK)),
            pl.BlockSpec((tm, MLA_KV_RANK), lambda i: (i, Z_CKV // MLA_KV_RANK)),
            pl.BlockSpec((tm, TAIL_W), lambda i: (i, Z_TAIL // TAIL_W)),
            pl.BlockSpec((tm, MLA_ROPE), lambda i: (i % n_pos_blocks, 0)),
            pl.BlockSpec((tm, MLA_ROPE), lambda i: (i % n_pos_blocks, 0)),
            pl.BlockSpec((1, MLA_Q_RANK), lambda i: (0, 0)),
            pl.BlockSpec((1, MLA_KV_RANK), lambda i: (0, 0)),
            pl.BlockSpec(wqb.shape, lambda i: (0, 0)),
            pl.BlockSpec(wkb_t.shape, lambda i: (0, 0, 0)),
        ],
        out_specs=[
            pl.BlockSpec((MLA_HEADS, tm, QK_PAD), lambda i: (0, i, 0)),
            pl.BlockSpec((tm, MLA_KV_RANK), lambda i: (i, 0)),
            pl.BlockSpec((tm, MLA_ROPE), lambda i: (i, 0)),
            pl.BlockSpec((tm, QK_PAD), lambda i: (i, 0)),
        ],
        out_shape=[
            jax.ShapeDtypeStruct((MLA_HEADS, m, QK_PAD), BF16),
            jax.ShapeDtypeStruct((m, MLA_KV_RANK), F32),
            jax.ShapeDtypeStruct((m, MLA_ROPE), F32),
            jax.ShapeDtypeStruct((m, QK_PAD), BF16),
        ],
        compiler_params=_cparams("parallel"),
        name="mla_prep",
    )(z, z, z, cos, sin, q_a_norm.reshape(1, -1), kv_a_norm.reshape(1, -1), wqb, wkb_t)


def _mla_prompt_kernel(q_ref, kv_ref, wvb_ref, o_ref, olat_ref, *, tq, nq):
    qi = pl.program_id(1)
    t = lax.broadcasted_iota(jnp.int32, (tq, tq), 0)
    c = lax.broadcasted_iota(jnp.int32, (tq, tq), 1)
    causal = c <= t

    for v in range(nq):
        n_past = v * tq

        @pl.when(qi == v)
        def _(n_past=n_past):
            def head(h, carry):
                q = q_ref[h]
                kv_d = kv_ref[n_past:n_past + tq, :]
                s_d = jnp.where(causal, _dot_nt(q, kv_d), -jnp.inf)
                m = jnp.max(s_d, axis=-1, keepdims=True)
                if n_past:
                    s_p = _dot_nt(q, kv_ref[0:n_past, :])
                    m = jnp.maximum(m, jnp.max(s_p, axis=-1, keepdims=True))
                p_d = jnp.exp(s_d - m)
                l = jnp.sum(p_d, axis=-1, keepdims=True)
                o = _dot(p_d.astype(BF16), kv_d[:, 0:MLA_KV_RANK])
                if n_past:
                    p_p = jnp.exp(s_p - m)
                    l = l + jnp.sum(p_p, axis=-1, keepdims=True)
                    o = o + _dot(p_p.astype(BF16), kv_ref[0:n_past, 0:MLA_KV_RANK])
                olat_ref[h] = o / l
                return carry

            lax.fori_loop(0, MLA_HEADS, head, 0, unroll=4)

    for h in range(MLA_HEADS):
        o_ref[:, h * MLA_DV:(h + 1) * MLA_DV] = _dot(olat_ref[h].astype(BF16), wvb_ref[h]).astype(BF16)


def mla_prompt_attention(q, kvcat, wvb, batch, seq, tq):
    nq = seq // tq
    return pl.pallas_call(
        functools.partial(_mla_prompt_kernel, tq=tq, nq=nq),
        grid=(batch, nq),
        in_specs=[
            pl.BlockSpec((MLA_HEADS, tq, QK_PAD), lambda b, i: (0, b * nq + i, 0)),
            pl.BlockSpec((seq, QK_PAD), lambda b, i: (b, 0)),
            pl.BlockSpec(wvb.shape, lambda b, i: (0, 0, 0)),
        ],
        out_specs=pl.BlockSpec((tq, MLA_HEADS * MLA_DV), lambda b, i: (b * nq + i, 0)),
        out_shape=jax.ShapeDtypeStruct((batch * seq, MLA_HEADS * MLA_DV), BF16),
        scratch_shapes=[pltpu.VMEM((MLA_HEADS, tq, MLA_KV_RANK), F32)],
        compiler_params=_cparams("parallel", "arbitrary"),
        name="mla_prompt_attn",
    )(q, kvcat, wvb)


ROWS_PER_STEP = 2


def _mla_decode_kernel(pt_ref, q_ref, kvn_ref, wvb_ref, ckv_hbm, kpet_hbm, o_ref,
                       ckv_buf, kpet_buf, sem, olat_ref, *, layer, page, n_pages):
    i = pl.program_id(0)
    n = pl.num_programs(0)

    def copies(row, slot):
        out = []
        for j in range(n_pages):
            pg = pt_ref[row, j]
            keys = pl.ds(j * page, page)
            out.append(pltpu.make_async_copy(ckv_hbm.at[layer, pg], ckv_buf.at[slot, keys], sem.at[slot]))
            out.append(pltpu.make_async_copy(kpet_hbm.at[layer, pg], kpet_buf.at[slot, :, keys], sem.at[slot]))
        return out

    def attend(r, slot):
        q = q_ref[r]
        kvn = kvn_ref[r]
        ckv = ckv_buf[slot].astype(BF16)
        kpet = kpet_buf[slot].astype(BF16)
        s = _dot_nt(q[:, 0:MLA_KV_RANK], ckv) + _dot(q[:, MLA_KV_RANK:QK_WIDTH], kpet)
        s_new = jnp.sum(q.astype(F32) * kvn.astype(F32), axis=-1, keepdims=True)
        m = jnp.maximum(jnp.max(s, axis=-1, keepdims=True), s_new)
        p = jnp.exp(s - m)
        p_new = jnp.exp(s_new - m)
        l = jnp.sum(p, axis=-1, keepdims=True) + p_new
        acc = _dot(p.astype(BF16), ckv) + p_new * kvn[:, 0:MLA_KV_RANK].astype(F32)
        o_lat = acc / l
        row = i * ROWS_PER_STEP + r
        for h in range(MLA_HEADS):
            olat_ref[h, pl.ds(row, 1), :] = o_lat[h:h + 1]

    @pl.when(i == 0)
    def _():
        for cp in copies(0, 0):
            cp.start()

    for r in range(ROWS_PER_STEP):
        row = i * ROWS_PER_STEP + r
        if r + 1 < ROWS_PER_STEP:
            for cp in copies(row + 1, r + 1):
                cp.start()
        else:
            @pl.when(i + 1 < n)
            def _():
                for cp in copies(row + 1, 0):
                    cp.start()
        for cp in copies(row, r):
            cp.wait()
        attend(r, r)

    @pl.when(i == n - 1)
    def _():
        for h in range(MLA_HEADS):
            o_ref[:, h * MLA_DV:(h + 1) * MLA_DV] = _dot(olat_ref[h].astype(BF16), wvb_ref[h]).astype(BF16)


def mla_decode_attention(page_table, q, kvcat_new, wvb, cache_ckv, cache_kpe_t, layer):
    nb, n_pages = page_table.shape
    page = cache_ckv.shape[2]
    assert nb % ROWS_PER_STEP == 0
    keys = n_pages * page
    grid_spec = pltpu.PrefetchScalarGridSpec(
        num_scalar_prefetch=1,
        grid=(nb // ROWS_PER_STEP,),
        in_specs=[
            pl.BlockSpec((ROWS_PER_STEP, MLA_HEADS, QK_PAD), lambda i, pt: (i, 0, 0)),
            pl.BlockSpec((ROWS_PER_STEP, 1, QK_PAD), lambda i, pt: (i, 0, 0)),
            pl.BlockSpec(wvb.shape, lambda i, pt: (0, 0, 0)),
            pl.BlockSpec(memory_space=pl.ANY),
            pl.BlockSpec(memory_space=pl.ANY),
        ],
        out_specs=pl.BlockSpec((nb, MLA_HEADS * MLA_DV), lambda i, pt: (0, 0)),
        scratch_shapes=[
            pltpu.VMEM((ROWS_PER_STEP, keys, MLA_KV_RANK), F32),
            pltpu.VMEM((ROWS_PER_STEP, MLA_ROPE, keys), F32),
            pltpu.SemaphoreType.DMA((ROWS_PER_STEP,)),
            pltpu.VMEM((MLA_HEADS, nb, MLA_KV_RANK), F32),
        ],
    )
    return pl.pallas_call(
        functools.partial(_mla_decode_kernel, layer=layer, page=page, n_pages=n_pages),
        grid_spec=grid_spec,
        out_shape=jax.ShapeDtypeStruct((nb, MLA_HEADS * MLA_DV), BF16),
        compiler_params=_cparams("arbitrary"),
        name="mla_decode_attn",
    )(page_table, q, kvcat_new.reshape(nb, 1, QK_PAD), wvb, cache_ckv, cache_kpe_t)


def _gla_prompt_kernel(q_ref, k_ref, v_ref, g_ref, tail_ref, wgk_ref, bgk_ref, gn_ref,
                       o_ref, sfin_ref, st_ref, *, ct):
    ci = pl.program_id(2)

    @pl.when(ci == 0)
    def _():
        st_ref[...] = jnp.zeros_like(st_ref)

    pre = _dot(tail_ref[...].astype(BF16), wgk_ref[...]) + bgk_ref[...]
    gk_all = _log_sigmoid(pre) * (1.0 / GLA_GATE_NORM)
    row = lax.broadcasted_iota(jnp.int32, (GLA_CHUNK, GLA_CHUNK), 0)
    col = lax.broadcasted_iota(jnp.int32, (GLA_CHUNK, GLA_CHUNK), 1)
    causal = col <= row
    tri = causal.astype(F32)
    gn = gn_ref[...]

    for c in range(ct // GLA_CHUNK):
        rows = slice(c * GLA_CHUNK, (c + 1) * GLA_CHUNK)
        for hh in range(GLA_HEADS_PER_STEP):
            ks = slice(hh * GLA_DK, (hh + 1) * GLA_DK)
            vs = slice(hh * GLA_DV, (hh + 1) * GLA_DV)
            b = jnp.dot(tri, gk_all[rows, ks], preferred_element_type=F32, precision=lax.Precision.HIGHEST)
            b_last = b[GLA_CHUNK - 1:GLA_CHUNK]
            q = q_ref[rows, ks] * (GLA_DK ** -0.5)
            k = k_ref[rows, ks]
            v = v_ref[rows, vs].astype(BF16)
            st = st_ref[hh]
            o = _dot_nt((q * jnp.exp(b)).astype(BF16), st.astype(BF16))
            parts = []
            for i in range(GLA_CHUNK // GLA_SUB):
                sub = slice(i * GLA_SUB, (i + 1) * GLA_SUB)
                seen = (i + 1) * GLA_SUB
                r = b[i * GLA_SUB:i * GLA_SUB + 1]
                q_i = (q[sub] * jnp.exp(b[sub] - r)).astype(BF16)
                k_i = (k[:seen] * jnp.exp(r - b[:seen])).astype(BF16)
                if seen < GLA_CHUNK:
                    k_i = jnp.concatenate([k_i, jnp.zeros((GLA_CHUNK - seen, GLA_DK), BF16)], axis=0)
                parts.append(_dot_nt(q_i, k_i))
            attn = jnp.where(causal, jnp.concatenate(parts, axis=0), 0.0)
            o = o + _dot(attn.astype(BF16), v)
            k_hat = (k * jnp.exp(b_last - b)).astype(BF16)
            st_ref[hh] = st * jnp.exp(b_last) + _dot_tn(v, k_hat)
            o = _rmsnorm(o, gn)
            g = g_ref[rows, vs]
            o_ref[rows, vs] = (o * (g * _sigmoid(g))).astype(BF16)

    @pl.when(ci == pl.num_programs(2) - 1)
    def _():
        for hh in range(GLA_HEADS_PER_STEP):
            sfin_ref[0, hh] = st_ref[hh].T


def gla_prompt(z, wgk, bgk, gla_norm, batch, seq, ct):
    m = z.shape[0]
    nc = seq // ct
    hp = GLA_HEADS_PER_STEP
    kw, vw = hp * GLA_DK, hp * GLA_DV
    row = lambda b, h, c: b * nc + c
    return pl.pallas_call(
        functools.partial(_gla_prompt_kernel, ct=ct),
        grid=(batch, GLA_HEADS // hp, nc),
        in_specs=[
            pl.BlockSpec((ct, kw), lambda b, h, c: (row(b, h, c), Z_Q // kw + h)),
            pl.BlockSpec((ct, kw), lambda b, h, c: (row(b, h, c), Z_K // kw + h)),
            pl.BlockSpec((ct, vw), lambda b, h, c: (row(b, h, c), Z_V // vw + h)),
            pl.BlockSpec((ct, vw), lambda b, h, c: (row(b, h, c), Z_G // vw + h)),
            pl.BlockSpec((ct, TAIL_W), lambda b, h, c: (row(b, h, c), Z_TAIL // TAIL_W)),
            pl.BlockSpec((TAIL_W, kw), lambda b, h, c: (0, h)),
            pl.BlockSpec((1, kw), lambda b, h, c: (0, h)),
            pl.BlockSpec((1, GLA_DV), lambda b, h, c: (0, 0)),
        ],
        out_specs=[
            pl.BlockSpec((ct, vw), lambda b, h, c: (row(b, h, c), h)),
            pl.BlockSpec((1, hp, GLA_DK, GLA_DV), lambda b, h, c: (b, h, 0, 0)),
        ],
        out_shape=[
            jax.ShapeDtypeStruct((m, GLA_HEADS * GLA_DV), BF16),
            jax.ShapeDtypeStruct((batch, GLA_HEADS, GLA_DK, GLA_DV), F32),
        ],
        scratch_shapes=[pltpu.VMEM((hp, GLA_DV, GLA_DK), F32)],
        compiler_params=_cparams("parallel", "parallel", "arbitrary"),
        name="gla_prompt",
    )(z, z, z, z, z, wgk, bgk.reshape(1, -1), gla_norm.reshape(1, -1))


def _gla_decode_kernel(q_ref, k_ref, v_ref, g_ref, tail_ref, wgk_ref, bgk_ref, gn_ref, s_ref,
                       *rest, tb, layer, creates):
    if creates:
        o_ref, stack_ref = rest
        snew_ref = stack_ref.at[layer]
        for other in range(stack_ref.shape[0]):
            if other != layer:
                stack_ref[other] = jnp.zeros(stack_ref.shape[1:], F32)
    else:
        _, o_ref, snew_ref = rest
    pre = _dot(tail_ref[...].astype(BF16), wgk_ref[...]) + bgk_ref[...]
    decay = jnp.exp(_log_sigmoid(pre) * (1.0 / GLA_GATE_NORM))
    q_all = q_ref[...] * (GLA_DK ** -0.5)
    k_all = k_ref[...]
    v_all = v_ref[...]
    g_all = g_ref[...]
    gn = gn_ref[...]
    eye = (lax.broadcasted_iota(jnp.int32, (GLA_DK, GLA_DK), 0)
           == lax.broadcasted_iota(jnp.int32, (GLA_DK, GLA_DK), 1))

    def column(r):
        return jnp.sum(jnp.where(eye, r, 0.0), axis=1, keepdims=True)

    for t in range(tb):
        for h in range(GLA_HEADS):
            ks = slice(h * GLA_DK, (h + 1) * GLA_DK)
            vs = slice(h * GLA_DV, (h + 1) * GLA_DV)
            q_c = column(q_all[t:t + 1, ks])
            k_c = column(k_all[t:t + 1, ks])
            d_c = column(decay[t:t + 1, ks])
            s_new = d_c * s_ref[t, h] + k_c * v_all[t:t + 1, vs]
            snew_ref[t, h] = s_new
            o = jnp.sum(q_c * s_new, axis=0, keepdims=True)
            o = _rmsnorm(o, gn)
            g = g_all[t:t + 1, vs]
            o_ref[t:t + 1, vs] = (o * (g * _sigmoid(g))).astype(BF16)


def gla_decode(z, wgk, bgk, gla_norm, state_all, new_stack, layer, tb):
    m = z.shape[0]
    depth = state_all.shape[0]
    qk_w = GLA_HEADS * GLA_DK
    v_w = GLA_HEADS * GLA_DV
    st = (tb, GLA_HEADS, GLA_DK, GLA_DV)
    creates = new_stack is None
    in_specs = [
        pl.BlockSpec((tb, qk_w), lambda i: (i, Z_Q // qk_w)),
        pl.BlockSpec((tb, qk_w), lambda i: (i, Z_K // qk_w)),
        pl.BlockSpec((tb, v_w), lambda i: (i, Z_V // v_w)),
        pl.BlockSpec((tb, v_w), lambda i: (i, Z_G // v_w)),
        pl.BlockSpec((tb, TAIL_W), lambda i: (i, Z_TAIL // TAIL_W)),
        pl.BlockSpec((TAIL_W, qk_w), lambda i: (0, 0)),
        pl.BlockSpec((1, qk_w), lambda i: (0, 0)),
        pl.BlockSpec((1, GLA_DV), lambda i: (0, 0)),
        pl.BlockSpec((None,) + st, lambda i: (layer, i, 0, 0, 0)),
    ]
    args = [z, z, z, z, z, wgk, bgk.reshape(1, -1), gla_norm.reshape(1, -1), state_all]
    if creates:
        stack_spec = pl.BlockSpec((depth,) + st, lambda i: (0, i, 0, 0, 0))
        aliases = {}
    else:
        in_specs.append(pl.BlockSpec(memory_space=pl.ANY))
        args.append(new_stack)
        stack_spec = pl.BlockSpec((None,) + st, lambda i: (layer, i, 0, 0, 0))
        aliases = {len(args) - 1: 1}
    return pl.pallas_call(
        functools.partial(_gla_decode_kernel, tb=tb, layer=layer, creates=creates),
        grid=(m // tb,),
        in_specs=in_specs,
        out_specs=[pl.BlockSpec((tb, v_w), lambda i: (i, 0)), stack_spec],
        out_shape=[
            jax.ShapeDtypeStruct((m, v_w), BF16),
            jax.ShapeDtypeStruct(state_all.shape, F32),
        ],
        input_output_aliases=aliases,
        compiler_params=_cparams("parallel"),
        name="gla_decode",
    )(*args)


def _rot_cols(w):
    half = w.shape[-1] // 2
    return jnp.concatenate([-w[..., half:], w[..., :half]], axis=-1)


def _prep_layer_weights(w_in, w_gk2, w_qb, w_kb, w_vb):
    d = w_in.shape[0]
    qk = GLA_HEADS * GLA_DK
    vw = GLA_HEADS * GLA_DV
    o_q, o_k, o_v = 0, qk, 2 * qk
    o_glr = o_v + vw
    o_g = o_glr + GLA_GATE_RANK
    o_cq = o_g + vw
    o_ckv = o_cq + MLA_Q_RANK
    o_kpe = o_ckv + MLA_KV_RANK
    w_t = jnp.swapaxes(w_in, 0, 1)
    half = MLA_ROPE // 2
    pad = jnp.zeros((TAIL_W - 2 * MLA_ROPE - GLA_GATE_RANK, d), F32)
    w_in_r = jnp.concatenate([
        w_t[o_q:o_q + qk], w_t[o_k:o_k + qk], w_t[o_v:o_v + vw], w_t[o_g:o_g + vw],
        w_t[o_cq:o_cq + MLA_Q_RANK], w_t[o_ckv:o_ckv + MLA_KV_RANK],
        w_t[o_kpe:o_kpe + MLA_ROPE], -w_t[o_kpe + half:o_kpe + MLA_ROPE], w_t[o_kpe:o_kpe + half],
        w_t[o_glr:o_glr + GLA_GATE_RANK], pad], axis=0).astype(BF16)
    wgk = jnp.zeros((TAIL_W, qk), F32).at[TAIL_GLR:TAIL_GLR + GLA_GATE_RANK].set(w_gk2).astype(BF16)
    wq = w_qb.reshape(MLA_Q_RANK, MLA_HEADS, MLA_NOPE + MLA_ROPE)
    wq_pe = wq[..., MLA_NOPE:]
    wqb_r = jnp.concatenate([wq[..., :MLA_NOPE], wq_pe, _rot_cols(wq_pe)], axis=-1)
    wqb_r = wqb_r.reshape(MLA_Q_RANK, MLA_HEADS * Q_HEAD_COLS).astype(BF16)
    wkb_t = jnp.transpose(w_kb, (1, 2, 0)).astype(BF16)
    wvb_h = jnp.transpose(w_vb, (1, 0, 2)).astype(BF16)
    return dict(w_in=w_in_r, wgk=wgk, wqb=wqb_r, wkb=wkb_t, wvb=wvb_h)


def _rope_tables(pos):
    freqs = ROPE_THETA ** (-jnp.arange(0, MLA_ROPE, 2, dtype=F32) / MLA_ROPE)
    ang = pos[:, None] * freqs[None, :]
    cos, sin = jnp.cos(ang), jnp.sin(ang)
    return jnp.concatenate([cos, cos], axis=-1), jnp.concatenate([sin, sin], axis=-1)


def _pick(m, target):
    return min(m, target)


def kernel(x_prompt, x_sample, state_gla, cache_ckv, cache_kpe, page_table, p_prompt, p_sample,
           norm_mix, w_in, w_gk2, b_gk, gla_norm, q_a_norm, kv_a_norm, w_qb, w_kb, w_vb, w_o,
           norm_ffn, w_ffn_in, w_ffn_out, norm_ple, w_ple_gate, w_ple_proj, norm_final):
    bsz, seq, d = x_prompt.shape
    nb, ls, _ = x_sample.shape
    assert ls == 1
    depth = w_in.shape[0]
    past = page_table.shape[1] * cache_ckv.shape[2]
    mp = bsz * seq
    tm_p = _pick(mp, 1024)

    cache_kpe_t = jnp.swapaxes(cache_kpe, 2, 3)
    cos_p, sin_p = _rope_tables(jnp.arange(seq, dtype=F32))
    cos_s, sin_s = _rope_tables(jnp.full((nb,), float(past), F32))

    xp = x_prompt.reshape(mp, d)
    xs = x_sample.reshape(nb, d)
    gla_p, ckv_p, kpe_p, ckv_s, kpe_s = [], [], [], [], []
    gla_s = None
    for i in range(depth):
        w = _prep_layer_weights(w_in[i], w_gk2[i], w_qb[i], w_kb[i], w_vb[i])
        z = norm_matmul(xp, norm_mix[i], w["w_in"], tm_p, 1024)
        mix_a, s_fin = gla_prompt(z, w["wgk"], b_gk[i], gla_norm[i], bsz, seq, _pick(seq, 512))
        q, ckv, kpe, kvcat = mla_prep(z, cos_p, sin_p, q_a_norm[i], kv_a_norm[i], w["wqb"], w["wkb"],
                                      _pick(seq, 512))
        mix_b = mla_prompt_attention(q, kvcat, w["wvb"], bsz, seq, _pick(seq, 512))
        gla_p.append(s_fin)
        ckv_p.append(ckv.reshape(bsz, seq, MLA_KV_RANK))
        kpe_p.append(kpe.reshape(bsz, seq, MLA_ROPE))
        zs = norm_matmul(xs, norm_mix[i], w["w_in"], nb, 1024)
        mix_a_s, gla_s = gla_decode(zs, w["wgk"], b_gk[i], gla_norm[i], state_gla, gla_s, i, 8)
        q_s, ckv_n, kpe_n, kvcat_n = mla_prep(zs, cos_s, sin_s, q_a_norm[i], kv_a_norm[i], w["wqb"],
                                              w["wkb"], nb)
        q_s = jnp.transpose(q_s, (1, 0, 2))
        mix_b_s = mla_decode_attention(page_table, q_s, kvcat_n, w["wvb"], cache_ckv, cache_kpe_t, i)
        ckv_s.append(ckv_n.reshape(nb, ls, MLA_KV_RANK))
        kpe_s.append(kpe_n.reshape(nb, ls, MLA_ROPE))
        (xp, hp), (xs, hs) = out_proj(xp, mix_a, mix_b, xs, mix_a_s, mix_b_s, norm_ffn[i], w_o, i, _pick(mp, 512))
        act, act_s = ffn_in(hp, hs, w_ffn_in, i, tm_p, 512)
        (xp, hp), (xs, hs) = ffn_out(xp, act, xs, act_s, norm_ple[i], w_ffn_out, i, _pick(mp, 256))
        (xp,), (xs,) = ple(xp, hp, p_prompt[i].reshape(mp, -1), xs, hs, p_sample[i].reshape(nb, -1),
                           w_ple_gate, w_ple_proj[i], norm_final, i, i == depth - 1, _pick(mp, 512))

    y_prompt = xp.reshape(bsz, seq, d)
    y_sample = xs.reshape(nb, ls, d)
    return (y_prompt, y_sample, jnp.stack(gla_p), gla_s, jnp.stack(ckv_p),
            jnp.stack(kpe_p), jnp.stack(ckv_s), jnp.stack(kpe_s))
```

```python
import functools

import jax
import jax.numpy as jnp
import numpy as np
from jax import lax
from jax.experimental import pallas as pl
from jax.experimental.pallas import tpu as pltpu

F32 = jnp.float32
BF16 = jnp.bfloat16

GLA_HEADS = 4
GLA_DK = 128
GLA_DV = 256
GLA_GATE_RANK = 16
GLA_GATE_NORM = 16.0
MLA_HEADS = 8
MLA_Q_RANK = 512
MLA_KV_RANK = 256
MLA_NOPE = 128
MLA_ROPE = 64
MLA_DV = 128
ROPE_THETA = 10000.0
EPS = 1e-6

QK_WIDTH = MLA_KV_RANK + MLA_ROPE
QK_PAD = 384
Q_HEAD_COLS = MLA_NOPE + 2 * MLA_ROPE

Z_Q = 0
Z_K = Z_Q + GLA_HEADS * GLA_DK
Z_V = Z_K + GLA_HEADS * GLA_DK
Z_G = Z_V + GLA_HEADS * GLA_DV
Z_CQ = Z_G + GLA_HEADS * GLA_DV
Z_CKV = Z_CQ + MLA_Q_RANK
Z_TAIL = Z_CKV + MLA_KV_RANK
TAIL_W = 256
TAIL_KPE = 0
TAIL_KPE_ROT = MLA_ROPE
TAIL_GLR = 2 * MLA_ROPE
Z_WIDTH = Z_TAIL + TAIL_W

VMEM_LIMIT_BYTES = 56 * 1024 * 1024

GLA_CHUNK = 128
GLA_SUB = 16
GLA_HEADS_PER_STEP = 2


def _cparams(*sem):
    return pltpu.CompilerParams(dimension_semantics=sem, vmem_limit_bytes=VMEM_LIMIT_BYTES)


def _rmsnorm(x, g):
    ms = jnp.mean(x * x, axis=-1, keepdims=True)
    return x * lax.rsqrt(ms + EPS) * g


def _sigmoid(x):
    return 1.0 / (1.0 + jnp.exp(-x))


def _log_sigmoid(x):
    return jnp.minimum(x, 0.0) - jnp.log1p(jnp.exp(-jnp.abs(x)))


def _dot(a, b):
    return jnp.dot(a, b, preferred_element_type=F32)


def _dot_nt(a, b):
    return lax.dot_general(a, b, (((1,), (1,)), ((), ())), preferred_element_type=F32)


def _dot_tn(a, b):
    return lax.dot_general(a, b, (((0,), (0,)), ((), ())), preferred_element_type=F32)


def _norm_mm_kernel(x_ref, g_ref, w_ref, o_ref, h_ref):
    @pl.when(pl.program_id(1) == 0)
    def _():
        h_ref[...] = _rmsnorm(x_ref[...], g_ref[...]).astype(BF16)

    o_ref[...] = _dot_nt(h_ref[...], w_ref[...])


def norm_matmul(x, g, w_t, tm, tn):
    m, k = x.shape
    n = w_t.shape[0]
    return pl.pallas_call(
        _norm_mm_kernel,
        grid=(m // tm, n // tn),
        in_specs=[
            pl.BlockSpec((tm, k), lambda i, j: (i, 0)),
            pl.BlockSpec((1, k), lambda i, j: (0, 0)),
            pl.BlockSpec((tn, k), lambda i, j: (j, 0)),
        ],
        out_specs=pl.BlockSpec((tm, tn), lambda i, j: (i, j)),
        out_shape=jax.ShapeDtypeStruct((m, n), F32),
        scratch_shapes=[pltpu.VMEM((tm, k), BF16)],
        compiler_params=_cparams("parallel", "arbitrary"),
        name="in_proj",
    )(x, g.reshape(1, k), w_t)


STAGE_ROWS = 256


def _stage_cast(w_hbm, stage_ref, sem, dst_ref):
    n = dst_ref.shape[0] // STAGE_ROWS

    def copy(c):
        return pltpu.make_async_copy(w_hbm.at[pl.ds(c * STAGE_ROWS, STAGE_ROWS)], stage_ref.at[c % 2], sem.at[c % 2])

    copy(0).start()
    for c in range(n):
        if c + 1 < n:
            copy(c + 1).start()
        copy(c).wait()
        dst_ref[c * STAGE_ROWS:(c + 1) * STAGE_ROWS, :] = stage_ref[c % 2].astype(BF16)


def _resident_call(body, row_args, sample_args, consts, w_stack, layer, outs, tm, name):
    m = row_args[0].shape[0]
    ms = sample_args[0].shape[0]
    n_row, n_const, n_out = len(row_args), len(consts), len(outs)
    wk, wn = w_stack.shape[1:]

    def kern(*refs):
        rows = refs[0:n_row]
        srows = refs[n_row:2 * n_row]
        cs = refs[2 * n_row:2 * n_row + n_const]
        w_hbm = refs[2 * n_row + n_const]
        o = refs[2 * n_row + n_const + 1:][:n_out]
        so = refs[2 * n_row + n_const + 1 + n_out:][:n_out]
        w_ref, stage_ref, sem = refs[-3:]
        i = pl.program_id(0)

        @pl.when(i == 0)
        def _():
            _stage_cast(w_hbm.at[layer], stage_ref, sem, w_ref)

        for r, val in zip(o, body(w_ref, *[c[...] for c in cs], *[r[...] for r in rows])):
            r[...] = val.astype(r.dtype)

        @pl.when(i == pl.num_programs(0) - 1)
        def _():
            for r, val in zip(so, body(w_ref, *[c[...] for c in cs], *[r[...] for r in srows])):
                r[...] = val.astype(r.dtype)

    in_specs = ([pl.BlockSpec((tm, a.shape[1]), lambda i: (i, 0)) for a in row_args]
                + [pl.BlockSpec(a.shape, lambda i: (0, 0)) for a in sample_args]
                + [pl.BlockSpec(c.shape, lambda i, nd=c.ndim: (0,) * nd) for c in consts]
                + [pl.BlockSpec(memory_space=pl.ANY)])
    out_specs = ([pl.BlockSpec((tm, w), lambda i: (i, 0)) for w, _ in outs]
                 + [pl.BlockSpec((ms, w), lambda i: (0, 0)) for w, _ in outs])
    out_shape = ([jax.ShapeDtypeStruct((m, w), dt) for w, dt in outs]
                 + [jax.ShapeDtypeStruct((ms, w), dt) for w, dt in outs])
    res = pl.pallas_call(
        kern,
        grid=(m // tm,),
        in_specs=in_specs,
        out_specs=out_specs,
        out_shape=out_shape,
        scratch_shapes=[pltpu.VMEM((wk, wn), BF16), pltpu.VMEM((2, STAGE_ROWS, wn), F32),
                        pltpu.SemaphoreType.DMA((2,))],
        compiler_params=_cparams("arbitrary"),
        name=name,
    )(*row_args, *sample_args, *consts, w_stack)
    return res[:n_out], res[n_out:]


def _out_proj_body(w_ref, g, x, mix_a, mix_b):
    half = mix_a.shape[1]
    y = x + _dot(mix_a, w_ref[0:half, :]) + _dot(mix_b, w_ref[half:2 * half, :])
    return y, _rmsnorm(y, g)


def out_proj(x, mix_a, mix_b, xs, mix_a_s, mix_b_s, norm_next, w_o, layer, tm):
    d = x.shape[1]
    return _resident_call(_out_proj_body, [x, mix_a, mix_b], [xs, mix_a_s, mix_b_s], [norm_next.reshape(1, d)],
                          w_o, layer, [(d, F32), (d, BF16)], tm, "out_proj")


def _ffn_out_body(w_ref, g, x, act):
    y = x + _dot(act, w_ref[...])
    return y, _rmsnorm(y, g)


def ffn_out(x, act, xs, act_s, norm_next, w_ffn_out, layer, tm):
    d = x.shape[1]
    return _resident_call(_ffn_out_body, [x, act], [xs, act_s], [norm_next.reshape(1, d)],
                          w_ffn_out, layer, [(d, F32), (d, BF16)], tm, "ffn_out")


def _ple_body(w_ref, wp, g_fin, x, h, p, *, final):
    gate = _sigmoid(_dot(h, w_ref[...]))
    emb = _dot(p.astype(BF16), wp.astype(BF16))
    y = x + gate * emb
    return (_rmsnorm(y, g_fin),) if final else (y,)


def ple(x, h, p, xs, hs, ps, w_ple_gate, w_ple_proj_l, norm_final, layer, final, tm):
    d = x.shape[1]
    return _resident_call(functools.partial(_ple_body, final=final), [x, h, p], [xs, hs, ps],
                          [w_ple_proj_l, norm_final.reshape(1, d)], w_ple_gate, layer, [(d, F32)], tm, "ple")


def _ffn_in_kernel(h_ref, hs_ref, wg_ref, wu_ref, o_ref, os_ref, wgb_ref, wub_ref):
    i = pl.program_id(1)

    @pl.when(i == 0)
    def _():
        wgb_ref[...] = wg_ref[...].astype(BF16)
        wub_ref[...] = wu_ref[...].astype(BF16)

    def swiglu(h):
        a = _dot(h, wgb_ref[...])
        u = _dot(h, wub_ref[...])
        return (a * _sigmoid(a) * u).astype(BF16)

    o_ref[...] = swiglu(h_ref[...])

    @pl.when(i == pl.num_programs(1) - 1)
    def _():
        os_ref[...] = swiglu(hs_ref[...])


def ffn_in(h, hs, w_ffn_in, layer, tm, tn):
    m, k = h.shape
    ms = hs.shape[0]
    d_ff = w_ffn_in.shape[2] // 2
    nj = d_ff // tn
    return pl.pallas_call(
        _ffn_in_kernel,
        grid=(nj, m // tm),
        in_specs=[
            pl.BlockSpec((tm, k), lambda j, i: (i, 0)),
            pl.BlockSpec((ms, k), lambda j, i: (0, 0)),
            pl.BlockSpec((None, k, tn), lambda j, i: (layer, 0, j)),
            pl.BlockSpec((None, k, tn), lambda j, i: (layer, 0, j + nj)),
        ],
        out_specs=[
            pl.BlockSpec((tm, tn), lambda j, i: (i, j)),
            pl.BlockSpec((ms, tn), lambda j, i: (0, j)),
        ],
        out_shape=[jax.ShapeDtypeStruct((m, d_ff), BF16), jax.ShapeDtypeStruct((ms, d_ff), BF16)],
        scratch_shapes=[pltpu.VMEM((k, tn), BF16), pltpu.VMEM((k, tn), BF16)],
        compiler_params=_cparams("arbitrary", "arbitrary"),
        name="ffn_in",
    )(h, hs, w_ffn_in, w_ffn_in)


def _mla_prep_kernel(cq_ref, ckv_ref, tail_ref, cos_ref, sin_ref, qn_ref, kvn_ref, wqb_ref, wkb_ref,
                     q_ref, ckv_out_ref, kpe_out_ref, kvcat_ref):
    scale = (MLA_NOPE + MLA_ROPE) ** -0.5
    cos = cos_ref[...]
    sin = sin_ref[...]
    cqn = _rmsnorm(cq_ref[...], qn_ref[...]).astype(BF16)
    qb = _dot(cqn, wqb_ref[...])
    for h in range(MLA_HEADS):
        base = h * Q_HEAD_COLS
        nope = qb[:, base:base + MLA_NOPE]
        pe = qb[:, base + MLA_NOPE:base + MLA_NOPE + MLA_ROPE]
        pe_rot = qb[:, base + MLA_NOPE + MLA_ROPE:base + Q_HEAD_COLS]
        q_lat = _dot(nope.astype(BF16), wkb_ref[h])
        q_pe = pe * cos + pe_rot * sin
        q_ref[h, :, 0:MLA_KV_RANK] = (q_lat * scale).astype(BF16)
        q_ref[h, :, MLA_KV_RANK:QK_WIDTH] = (q_pe * scale).astype(BF16)
        q_ref[h, :, QK_WIDTH:QK_PAD] = jnp.zeros((q_pe.shape[0], QK_PAD - QK_WIDTH), BF16)
    ckv = _rmsnorm(ckv_ref[...], kvn_ref[...])
    tail = tail_ref[...]
    kpe = (tail[:, TAIL_KPE:TAIL_KPE + MLA_ROPE] * cos
           + tail[:, TAIL_KPE_ROT:TAIL_KPE_ROT + MLA_ROPE] * sin)
    ckv_out_ref[...] = ckv
    kpe_out_ref[...] = kpe
    kvcat_ref[:, 0:MLA_KV_RANK] = ckv.astype(BF16)
    kvcat_ref[:, MLA_KV_RANK:QK_WIDTH] = kpe.astype(BF16)
    kvcat_ref[:, QK_WIDTH:QK_PAD] = jnp.zeros((kpe.shape[0], QK_PAD - QK_WIDTH), BF16)


def mla_prep(z, cos, sin, q_a_norm, kv_a_norm, wqb, wkb_t, tm):
    m = z.shape[0]
    n_pos_blocks = cos.shape[0] // tm
    return pl.pallas_call(
        _mla_prep_kernel,
        grid=(m // tm,),
        in_specs=[
            pl.BlockSpec((tm, MLA_Q_RANK), lambda i: (i, Z_CQ // MLA_Q_RANK)),
            pl.BlockSpec((tm, MLA_KV_RANK), lambda i: (i, Z_CKV // MLA_KV_RANK)),
            pl.BlockSpec((tm, TAIL_W), lambda i: (i, Z_TAIL // TAIL_W)),
            pl.BlockSpec((tm, MLA_ROPE), lambda i: (i % n_pos_blocks, 0)),
            pl.BlockSpec((tm, MLA_ROPE), lambda i: (i % n_pos_blocks, 0)),
            pl.BlockSpec((1, MLA_Q_RANK), lambda i: (0, 0)),
            pl.BlockSpec((1, MLA_KV_RANK), lambda i: (0, 0)),
            pl.BlockSpec(wqb.shape, lambda i: (0, 0)),
            pl.BlockSpec(wkb_t.shape, lambda i: (0, 0, 0)),
        ],
        out_specs=[
            pl.BlockSpec((MLA_HEADS, tm, QK_PAD), lambda i: (0, i, 0)),
            pl.BlockSpec((tm, MLA_KV_RANK), lambda i: (i, 0)),
            pl.BlockSpec((tm, MLA_ROPE), lambda i: (i, 0)),
            pl.BlockSpec((tm, QK_PAD), lambda i: (i, 0)),
        ],
        out_shape=[
            jax.ShapeDtypeStruct((MLA_HEADS, m, QK_PAD), BF16),
            jax.ShapeDtypeStruct((m, MLA_KV_RANK), F32),
            jax.ShapeDtypeStruct((m, MLA_ROPE), F32),
            jax.ShapeDtypeStruct((m, QK_PAD), BF16),
        ],
        compiler_params=_cparams("parallel"),
        name="mla_prep",
    )(z, z, z, cos, sin, q_a_norm.reshape(1, -1), kv_a_norm.reshape(1, -1), wqb, wkb_t)


def _mla_prompt_kernel(q_ref, kv_ref, wvb_ref, o_ref, olat_ref, *, tq, nq):
    qi = pl.program_id(1)
    t = lax.broadcasted_iota(jnp.int32, (tq, tq), 0)
    c = lax.broadcasted_iota(jnp.int32, (tq, tq), 1)
    causal = c <= t

    for v in range(nq):
        n_past = v * tq

        @pl.when(qi == v)
        def _(n_past=n_past):
            def head(h, carry):
                q = q_ref[h]
                kv_d = kv_ref[n_past:n_past + tq, :]
                s_d = jnp.where(causal, _dot_nt(q, kv_d), -jnp.inf)
                m = jnp.max(s_d, axis=-1, keepdims=True)
                if n_past:
                    s_p = _dot_nt(q, kv_ref[0:n_past, :])
                    m = jnp.maximum(m, jnp.max(s_p, axis=-1, keepdims=True))
                p_d = jnp.exp(s_d - m)
                l = jnp.sum(p_d, axis=-1, keepdims=True)
                o = _dot(p_d.astype(BF16), kv_d[:, 0:MLA_KV_RANK])
                if n_past:
                    p_p = jnp.exp(s_p - m)
                    l = l + jnp.sum(p_p, axis=-1, keepdims=True)
                    o = o + _dot(p_p.astype(BF16), kv_ref[0:n_past, 0:MLA_KV_RANK])
                olat_ref[h] = o / l
                return carry

            lax.fori_loop(0, MLA_HEADS, head, 0, unroll=4)

    for h in range(MLA_HEADS):
        o_ref[:, h * MLA_DV:(h + 1) * MLA_DV] = _dot(olat_ref[h].astype(BF16), wvb_ref[h]).astype(BF16)


def mla_prompt_attention(q, kvcat, wvb, batch, seq, tq):
    nq = seq // tq
    return pl.pallas_call(
        functools.partial(_mla_prompt_kernel, tq=tq, nq=nq),
        grid=(batch, nq),
        in_specs=[
            pl.BlockSpec((MLA_HEADS, tq, QK_PAD), lambda b, i: (0, b * nq + i, 0)),
            pl.BlockSpec((seq, QK_PAD), lambda b, i: (b, 0)),
            pl.BlockSpec(wvb.shape, lambda b, i: (0, 0, 0)),
        ],
        out_specs=pl.BlockSpec((tq, MLA_HEADS * MLA_DV), lambda b, i: (b * nq + i, 0)),
        out_shape=jax.ShapeDtypeStruct((batch * seq, MLA_HEADS * MLA_DV), BF16),
        scratch_shapes=[pltpu.VMEM((MLA_HEADS, tq, MLA_KV_RANK), F32)],
        compiler_params=_cparams("parallel", "arbitrary"),
        name="mla_prompt_attn",
    )(q, kvcat, wvb)


ROWS_PER_STEP = 2


def _mla_decode_kernel(pt_ref, q_ref, kvn_ref, wvb_ref, ckv_hbm, kpet_hbm, o_ref,
                       ckv_buf, kpet_buf, sem, olat_ref, *, layer, page, n_pages):
    i = pl.program_id(0)
    n = pl.num_programs(0)

    def copies(row, slot):
        out = []
        for j in range(n_pages):
            pg = pt_ref[row, j]
            keys = pl.ds(j * page, page)
            out.append(pltpu.make_async_copy(ckv_hbm.at[layer, pg], ckv_buf.at[slot, keys], sem.at[slot]))
            out.append(pltpu.make_async_copy(kpet_hbm.at[layer, pg], kpet_buf.at[slot, :, keys], sem.at[slot]))
        return out

    def attend(r, slot):
        q = q_ref[r]
        kvn = kvn_ref[r]
        ckv = ckv_buf[slot].astype(BF16)
        kpet = kpet_buf[slot].astype(BF16)
        s = _dot_nt(q[:, 0:MLA_KV_RANK], ckv) + _dot(q[:, MLA_KV_RANK:QK_WIDTH], kpet)
        s_new = jnp.sum(q.astype(F32) * kvn.astype(F32), axis=-1, keepdims=True)
        m = jnp.maximum(jnp.max(s, axis=-1, keepdims=True), s_new)
        p = jnp.exp(s - m)
        p_new = jnp.exp(s_new - m)
        l = jnp.sum(p, axis=-1, keepdims=True) + p_new
        acc = _dot(p.astype(BF16), ckv) + p_new * kvn[:, 0:MLA_KV_RANK].astype(F32)
        o_lat = acc / l
        row = i * ROWS_PER_STEP + r
        for h in range(MLA_HEADS):
            olat_ref[h, pl.ds(row, 1), :] = o_lat[h:h + 1]

    @pl.when(i == 0)
    def _():
        for cp in copies(0, 0):
            cp.start()

    for r in range(ROWS_PER_STEP):
        row = i * ROWS_PER_STEP + r
        if r + 1 < ROWS_PER_STEP:
            for cp in copies(row + 1, r + 1):
                cp.start()
        else:
            @pl.when(i + 1 < n)
            def _():
                for cp in copies(row + 1, 0):
                    cp.start()
        for cp in copies(row, r):
            cp.wait()
        attend(r, r)

    @pl.when(i == n - 1)
    def _():
        for h in range(MLA_HEADS):
            o_ref[:, h * MLA_DV:(h + 1) * MLA_DV] = _dot(olat_ref[h].astype(BF16), wvb_ref[h]).astype(BF16)


def mla_decode_attention(page_table, q, kvcat_new, wvb, cache_ckv, cache_kpe_t, layer):
    nb, n_pages = page_table.shape
    page = cache_ckv.shape[2]
    assert nb % ROWS_PER_STEP == 0
    keys = n_pages * page
    grid_spec = pltpu.PrefetchScalarGridSpec(
        num_scalar_prefetch=1,
        grid=(nb // ROWS_PER_STEP,),
        in_specs=[
            pl.BlockSpec((ROWS_PER_STEP, MLA_HEADS, QK_PAD), lambda i, pt: (i, 0, 0)),
            pl.BlockSpec((ROWS_PER_STEP, 1, QK_PAD), lambda i, pt: (i, 0, 0)),
            pl.BlockSpec(wvb.shape, lambda i, pt: (0, 0, 0)),
            pl.BlockSpec(memory_space=pl.ANY),
            pl.BlockSpec(memory_space=pl.ANY),
        ],
        out_specs=pl.BlockSpec((nb, MLA_HEADS * MLA_DV), lambda i, pt: (0, 0)),
        scratch_shapes=[
            pltpu.VMEM((ROWS_PER_STEP, keys, MLA_KV_RANK), F32),
            pltpu.VMEM((ROWS_PER_STEP, MLA_ROPE, keys), F32),
            pltpu.SemaphoreType.DMA((ROWS_PER_STEP,)),
            pltpu.VMEM((MLA_HEADS, nb, MLA_KV_RANK), F32),
        ],
    )
    return pl.pallas_call(
        functools.partial(_mla_decode_kernel, layer=layer, page=page, n_pages=n_pages),
        grid_spec=grid_spec,
        out_shape=jax.ShapeDtypeStruct((nb, MLA_HEADS * MLA_DV), BF16),
        compiler_params=_cparams("arbitrary"),
        name="mla_decode_attn",
    )(page_table, q, kvcat_new.reshape(nb, 1, QK_PAD), wvb, cache_ckv, cache_kpe_t)


def _gla_prompt_kernel(q_ref, k_ref, v_ref, g_ref, tail_ref, wgk_ref, bgk_ref, gn_ref,
                       o_ref, sfin_ref, st_ref, *, ct):
    ci = pl.program_id(2)

    @pl.when(ci == 0)
    def _():
        st_ref[...] = jnp.zeros_like(st_ref)

    pre = _dot(tail_ref[...].astype(BF16), wgk_ref[...]) + bgk_ref[...]
    gk_all = _log_sigmoid(pre) * (1.0 / GLA_GATE_NORM)
    row = lax.broadcasted_iota(jnp.int32, (GLA_CHUNK, GLA_CHUNK), 0)
    col = lax.broadcasted_iota(jnp.int32, (GLA_CHUNK, GLA_CHUNK), 1)
    causal = col <= row
    tri = causal.astype(F32)
    gn = gn_ref[...]

    for c in range(ct // GLA_CHUNK):
        rows = slice(c * GLA_CHUNK, (c + 1) * GLA_CHUNK)
        for hh in range(GLA_HEADS_PER_STEP):
            ks = slice(hh * GLA_DK, (hh + 1) * GLA_DK)
            vs = slice(hh * GLA_DV, (hh + 1) * GLA_DV)
            b = jnp.dot(tri, gk_all[rows, ks], preferred_element_type=F32, precision=lax.Precision.HIGHEST)
            b_last = b[GLA_CHUNK - 1:GLA_CHUNK]
            q = q_ref[rows, ks] * (GLA_DK ** -0.5)
            k = k_ref[rows, ks]
            v = v_ref[rows, vs].astype(BF16)
            st = st_ref[hh]
            o = _dot_nt((q * jnp.exp(b)).astype(BF16), st.astype(BF16))
            parts = []
            for i in range(GLA_CHUNK // GLA_SUB):
                sub = slice(i * GLA_SUB, (i + 1) * GLA_SUB)
                seen = (i + 1) * GLA_SUB
                r = b[i * GLA_SUB:i * GLA_SUB + 1]
                q_i = (q[sub] * jnp.exp(b[sub] - r)).astype(BF16)
                k_i = (k[:seen] * jnp.exp(r - b[:seen])).astype(BF16)
                if seen < GLA_CHUNK:
                    k_i = jnp.concatenate([k_i, jnp.zeros((GLA_CHUNK - seen, GLA_DK), BF16)], axis=0)
                parts.append(_dot_nt(q_i, k_i))
            attn = jnp.where(causal, jnp.concatenate(parts, axis=0), 0.0)
            o = o + _dot(attn.astype(BF16), v)
            k_hat = (k * jnp.exp(b_last - b)).astype(BF16)
            st_ref[hh] = st * jnp.exp(b_last) + _dot_tn(v, k_hat)
            o = _rmsnorm(o, gn)
            g = g_ref[rows, vs]
            o_ref[rows, vs] = (o * (g * _sigmoid(g))).astype(BF16)

    @pl.when(ci == pl.num_programs(2) - 1)
    def _():
        for hh in range(GLA_HEADS_PER_STEP):
            sfin_ref[0, hh] = st_ref[hh].T


def gla_prompt(z, wgk, bgk, gla_norm, batch, seq, ct):
    m = z.shape[0]
    nc = seq // ct
    hp = GLA_HEADS_PER_STEP
    kw, vw = hp * GLA_DK, hp * GLA_DV
    row = lambda b, h, c: b * nc + c
    return pl.pallas_call(
        functools.partial(_gla_prompt_kernel, ct=ct),
        grid=(batch, GLA_HEADS // hp, nc),
        in_specs=[
            pl.BlockSpec((ct, kw), lambda b, h, c: (row(b, h, c), Z_Q // kw + h)),
            pl.BlockSpec((ct, kw), lambda b, h, c: (row(b, h, c), Z_K // kw + h)),
            pl.BlockSpec((ct, vw), lambda b, h, c: (row(b, h, c), Z_V // vw + h)),
            pl.BlockSpec((ct, vw), lambda b, h, c: (row(b, h, c), Z_G // vw + h)),
            pl.BlockSpec((ct, TAIL_W), lambda b, h, c: (row(b, h, c), Z_TAIL // TAIL_W)),
            pl.BlockSpec((TAIL_W, kw), lambda b, h, c: (0, h)),
            pl.BlockSpec((1, kw), lambda b, h, c: (0, h)),
            pl.BlockSpec((1, GLA_DV), lambda b, h, c: (0, 0)),
        ],
        out_specs=[
            pl.BlockSpec((ct, vw), lambda b, h, c: (row(b, h, c), h)),
            pl.BlockSpec((1, hp, GLA_DK, GLA_DV), lambda b, h, c: (b, h, 0, 0)),
        ],
        out_shape=[
            jax.ShapeDtypeStruct((m, GLA_HEADS * GLA_DV), BF16),
            jax.ShapeDtypeStruct((batch, GLA_HEADS, GLA_DK, GLA_DV), F32),
        ],
        scratch_shapes=[pltpu.VMEM((hp, GLA_DV, GLA_DK), F32)],
        compiler_params=_cparams("parallel", "parallel", "arbitrary"),
        name="gla_prompt",
    )(z, z, z, z, z, wgk, bgk.reshape(1, -1), gla_norm.reshape(1, -1))


def _gla_decode_kernel(q_ref, k_ref, v_ref, g_ref, tail_ref, wgk_ref, bgk_ref, gn_ref, s_ref,
                       *rest, tb, layer, creates):
    if creates:
        o_ref, stack_ref = rest
        snew_ref = stack_ref.at[layer]
        for other in range(stack_ref.shape[0]):
            if other != layer:
                stack_ref[other] = jnp.zeros(stack_ref.shape[1:], F32)
    else:
        _, o_ref, snew_ref = rest
    pre = _dot(tail_ref[...].astype(BF16), wgk_ref[...]) + bgk_ref[...]
    decay = jnp.exp(_log_sigmoid(pre) * (1.0 / GLA_GATE_NORM))
    q_all = q_ref[...] * (GLA_DK ** -0.5)
    k_all = k_ref[...]
    v_all = v_ref[...]
    g_all = g_ref[...]
    gn = gn_ref[...]
    eye = (lax.broadcasted_iota(jnp.int32, (GLA_DK, GLA_DK), 0)
           == lax.broadcasted_iota(jnp.int32, (GLA_DK, GLA_DK), 1))

    def column(r):
        return jnp.sum(jnp.where(eye, r, 0.0), axis=1, keepdims=True)

    for t in range(tb):
        for h in range(GLA_HEADS):
            ks = slice(h * GLA_DK, (h + 1) * GLA_DK)
            vs = slice(h * GLA_DV, (h + 1) * GLA_DV)
            q_c = column(q_all[t:t + 1, ks])
            k_c = column(k_all[t:t + 1, ks])
            d_c = column(decay[t:t + 1, ks])
            s_new = d_c * s_ref[t, h] + k_c * v_all[t:t + 1, vs]
            snew_ref[t, h] = s_new
            o = jnp.sum(q_c * s_new, axis=0, keepdims=True)
            o = _rmsnorm(o, gn)
            g = g_all[t:t + 1, vs]
            o_ref[t:t + 1, vs] = (o * (g * _sigmoid(g))).astype(BF16)


def gla_decode(z, wgk, bgk, gla_norm, state_all, new_stack, layer, tb):
    m = z.shape[0]
    depth = state_all.shape[0]
    qk_w = GLA_HEADS * GLA_DK
    v_w = GLA_HEADS * GLA_DV
    st = (tb, GLA_HEADS, GLA_DK, GLA_DV)
    creates = new_stack is None
    in_specs = [
        pl.BlockSpec((tb, qk_w), lambda i: (i, Z_Q // qk_w)),
        pl.BlockSpec((tb, qk_w), lambda i: (i, Z_K // qk_w)),
        pl.BlockSpec((tb, v_w), lambda i: (i, Z_V // v_w)),
        pl.BlockSpec((tb, v_w), lambda i: (i, Z_G // v_w)),
        pl.BlockSpec((tb, TAIL_W), lambda i: (i, Z_TAIL // TAIL_W)),
        pl.BlockSpec((TAIL_W, qk_w), lambda i: (0, 0)),
        pl.BlockSpec((1, qk_w), lambda i: (0, 0)),
        pl.BlockSpec((1, GLA_DV), lambda i: (0, 0)),
        pl.BlockSpec((None,) + st, lambda i: (layer, i, 0, 0, 0)),
    ]
    args = [z, z, z, z, z, wgk, bgk.reshape(1, -1), gla_norm.reshape(1, -1), state_all]
    if creates:
        stack_spec = pl.BlockSpec((depth,) + st, lambda i: (0, i, 0, 0, 0))
        aliases = {}
    else:
        in_specs.append(pl.BlockSpec(memory_space=pl.ANY))
        args.append(new_stack)
        stack_spec = pl.BlockSpec((None,) + st, lambda i: (layer, i, 0, 0, 0))
        aliases = {len(args) - 1: 1}
    return pl.pallas_call(
        functools.partial(_gla_decode_kernel, tb=tb, layer=layer, creates=creates),
        grid=(m // tb,),
        in_specs=in_specs,
        out_specs=[pl.BlockSpec((tb, v_w), lambda i: (i, 0)), stack_spec],
        out_shape=[
            jax.ShapeDtypeStruct((m, v_w), BF16),
            jax.ShapeDtypeStruct(state_all.shape, F32),
        ],
        input_output_aliases=aliases,
        compiler_params=_cparams("parallel"),
        name="gla_decode",
    )(*args)


def _rot_cols(w):
    half = w.shape[-1] // 2
    return jnp.concatenate([-w[..., half:], w[..., :half]], axis=-1)


def _prep_layer_weights(w_in, w_gk2, w_qb, w_kb, w_vb):
    d = w_in.shape[0]
    qk = GLA_HEADS * GLA_DK
    vw = GLA_HEADS * GLA_DV
    o_q, o_k, o_v = 0, qk, 2 * qk
    o_glr = o_v + vw
    o_g = o_glr + GLA_GATE_RANK
    o_cq = o_g + vw
    o_ckv = o_cq + MLA_Q_RANK
    o_kpe = o_ckv + MLA_KV_RANK
    w_t = jnp.swapaxes(w_in, 0, 1)
    half = MLA_ROPE // 2
    pad = jnp.zeros((TAIL_W - 2 * MLA_ROPE - GLA_GATE_RANK, d), BF16)
    w_in_r = jnp.concatenate([
        w_t[o_q:o_q + qk], w_t[o_k:o_k + qk], w_t[o_v:o_v + vw], w_t[o_g:o_g + vw],
        w_t[o_cq:o_cq + MLA_Q_RANK], w_t[o_ckv:o_ckv + MLA_KV_RANK],
        w_t[o_kpe:o_kpe + MLA_ROPE], -w_t[o_kpe + half:o_kpe + MLA_ROPE], w_t[o_kpe:o_kpe + half],
        w_t[o_glr:o_glr + GLA_GATE_RANK], pad], axis=0)
    wgk = jnp.zeros((TAIL_W, qk), F32).at[TAIL_GLR:TAIL_GLR + GLA_GATE_RANK].set(w_gk2).astype(BF16)
    wq = w_qb.reshape(MLA_Q_RANK, MLA_HEADS, MLA_NOPE + MLA_ROPE)
    wq_pe = wq[..., MLA_NOPE:]
    wqb_r = jnp.concatenate([wq[..., :MLA_NOPE], wq_pe, _rot_cols(wq_pe)], axis=-1)
    wqb_r = wqb_r.reshape(MLA_Q_RANK, MLA_HEADS * Q_HEAD_COLS).astype(BF16)
    wkb_t = jnp.transpose(w_kb, (1, 2, 0)).astype(BF16)
    wvb_h = jnp.transpose(w_vb, (1, 0, 2)).astype(BF16)
    return dict(w_in=w_in_r, wgk=wgk, wqb=wqb_r, wkb=wkb_t, wvb=wvb_h)


def _rope_tables(pos):
    freqs = ROPE_THETA ** (-jnp.arange(0, MLA_ROPE, 2, dtype=F32) / MLA_ROPE)
    ang = pos[:, None] * freqs[None, :]
    cos, sin = jnp.cos(ang), jnp.sin(ang)
    return jnp.concatenate([cos, cos], axis=-1), jnp.concatenate([sin, sin], axis=-1)


def _pick(m, target):
    return min(m, target)


def kernel(x_prompt, x_sample, state_gla, cache_ckv, cache_kpe, page_table, p_prompt, p_sample,
           norm_mix, w_in, w_gk2, b_gk, gla_norm, q_a_norm, kv_a_norm, w_qb, w_kb, w_vb, w_o,
           norm_ffn, w_ffn_in, w_ffn_out, norm_ple, w_ple_gate, w_ple_proj, norm_final):
    bsz, seq, d = x_prompt.shape
    nb, ls, _ = x_sample.shape
    assert ls == 1
    depth = w_in.shape[0]
    past = page_table.shape[1] * cache_ckv.shape[2]
    mp = bsz * seq
    tm_p = _pick(mp, 1024)

    cache_kpe_t = jnp.swapaxes(cache_kpe, 2, 3)
    cos_p, sin_p = _rope_tables(jnp.arange(seq, dtype=F32))
    cos_s, sin_s = _rope_tables(jnp.full((nb,), float(past), F32))

    xp = x_prompt.reshape(mp, d)
    xs = x_sample.reshape(nb, d)
    gla_p, ckv_p, kpe_p, ckv_s, kpe_s = [], [], [], [], []
    gla_s = None
    w_in_b = w_in.astype(BF16)
    for i in range(depth):
        w = _prep_layer_weights(w_in_b[i], w_gk2[i], w_qb[i], w_kb[i], w_vb[i])
        z = norm_matmul(xp, norm_mix[i], w["w_in"], tm_p, 2048)
        mix_a, s_fin = gla_prompt(z, w["wgk"], b_gk[i], gla_norm[i], bsz, seq, _pick(seq, 512))
        q, ckv, kpe, kvcat = mla_prep(z, cos_p, sin_p, q_a_norm[i], kv_a_norm[i], w["wqb"], w["wkb"],
                                      _pick(seq, 512))
        mix_b = mla_prompt_attention(q, kvcat, w["wvb"], bsz, seq, _pick(seq, 512))
        gla_p.append(s_fin)
        ckv_p.append(ckv.reshape(bsz, seq, MLA_KV_RANK))
        kpe_p.append(kpe.reshape(bsz, seq, MLA_ROPE))
        zs = norm_matmul(xs, norm_mix[i], w["w_in"], nb, 1024)
        mix_a_s, gla_s = gla_decode(zs, w["wgk"], b_gk[i], gla_norm[i], state_gla, gla_s, i, 8)
        q_s, ckv_n, kpe_n, kvcat_n = mla_prep(zs, cos_s, sin_s, q_a_norm[i], kv_a_norm[i], w["wqb"],
                                              w["wkb"], nb)
        q_s = jnp.transpose(q_s, (1, 0, 2))
        mix_b_s = mla_decode_attention(page_table, q_s, kvcat_n, w["wvb"], cache_ckv, cache_kpe_t, i)
        ckv_s.append(ckv_n.reshape(nb, ls, MLA_KV_RANK))
        kpe_s.append(kpe_n.reshape(nb, ls, MLA_ROPE))
        (xp, hp), (xs, hs) = out_proj(xp, mix_a, mix_b, xs, mix_a_s, mix_b_s, norm_ffn[i], w_o, i, _pick(mp, 512))
        act, act_s = ffn_in(hp, hs, w_ffn_in, i, tm_p, 512)
        (xp, hp), (xs, hs) = ffn_out(xp, act, xs, act_s, norm_ple[i], w_ffn_out, i, _pick(mp, 256))
        (xp,), (xs,) = ple(xp, hp, p_prompt[i].reshape(mp, -1), xs, hs, p_sample[i].reshape(nb, -1),
                           w_ple_gate, w_ple_proj[i], norm_final, i, i == depth - 1, _pick(mp, 512))

    y_prompt = xp.reshape(bsz, seq, d)
    y_sample = xs.reshape(nb, ls, d)
    return (y_prompt, y_sample, jnp.stack(gla_p), gla_s, jnp.stack(ckv_p),
            jnp.stack(kpe_p), jnp.stack(ckv_s), jnp.stack(kpe_s))
```
